```python
import math
import jax, jax.numpy as jnp
from jax import lax
import numpy as np

D_MODEL = 2048
BATCH = 4
SEQ = 2048
DEPTH = 4
DEC_BATCH = 128
DEC_SEQ = 8
PAST_LEN = 16384
PAGE_SIZE = 128

C_CONV = D_MODEL // 2
CONV_K = 31
H_RET = 4
DK_RET = (D_MODEL // 2) // H_RET
DV_RET = DK_RET
R_QK = H_RET * DK_RET
R_V = H_RET * DV_RET
D_IN = 2 * C_CONV + 2 * R_QK + 2 * R_V
D_MIX = C_CONV + R_V
D_FF = 5632
N_EXPERTS = 8
TOP_K = 2
D_FF_EXPERT = 5632
RET_CHUNK = 128
ROPE_BASE = 10000.0
LN_EPS = 1e-5
ALPHA = (2 * DEPTH) ** 0.25
BETA = (8 * DEPTH) ** -0.25
N_DENSE = (DEPTH + 1) // 2
N_MOE = DEPTH // 2

kernel_name = 'hymba_conformer_retention_moe_step'


def layer_norm(x, g, b):
    xf = x.astype(jnp.float32)
    mu = jnp.mean(xf, axis=-1, keepdims=True)
    var = jnp.mean(jnp.square(xf - mu), axis=-1, keepdims=True)
    return ((xf - mu) * lax.rsqrt(var + LN_EPS)).astype(x.dtype) * g + b


def rotary(x, pos):
    half = x.shape[-1] // 2
    inv = 1.0 / (ROPE_BASE ** (jnp.arange(half, dtype=jnp.float32) / half))
    ang = pos.astype(jnp.float32)[:, None] * inv[None, :]
    cos = jnp.cos(ang)[None, :, None, :]
    sin = jnp.sin(ang)[None, :, None, :]
    xf = x.astype(jnp.float32)
    x1, x2 = xf[..., :half], xf[..., half:]
    return jnp.concatenate([x1 * cos - x2 * sin, x1 * sin + x2 * cos], axis=-1)


def retention(q, k, v, s0):
    B, H, T, DK = q.shape
    DV = v.shape[-1]
    chunk = math.gcd(T, RET_CHUNK)
    n = T // chunk
    gamma = 1.0 - 2.0 ** (-5.0 - jnp.arange(H, dtype=jnp.float32))
    log_g = jnp.log(gamma)
    idx = jnp.arange(chunk, dtype=jnp.float32)
    diff = idx[:, None] - idx[None, :]
    dmask = jnp.where(diff[None] >= 0, jnp.exp(diff[None] * log_g[:, None, None]), 0.0)
    q_dec = jnp.exp((idx[None, :] + 1.0) * log_g[:, None])[..., None]
    k_dec = jnp.exp((chunk - 1.0 - idx[None, :]) * log_g[:, None])[..., None]
    c_dec = jnp.exp(chunk * log_g)[:, None, None]

    def to_chunks(a):
        return jnp.moveaxis(a.astype(jnp.float32).reshape(B, H, n, chunk, a.shape[-1]), 2, 0)

    def step(S, inp):
        qc, kc, vc = inp
        scores = jnp.einsum('bhik,bhjk->bhij', qc, kc) * dmask
        o = jnp.einsum('bhij,bhjv->bhiv', scores, vc) + jnp.einsum('bhik,bhkv->bhiv', qc * q_dec, S)
        S = c_dec * S + jnp.einsum('bhjk,bhjv->bhkv', kc * k_dec, vc)
        return S, o

    S, o = lax.scan(step, s0.astype(jnp.float32), (to_chunks(q), to_chunks(k), to_chunks(v)))
    o = jnp.moveaxis(o, 0, 2).reshape(B, H, T, DV)
    return o, S


def mixer(x, w_in, conv_w, conv_b, cln_g, cln_b, gn_g, gn_b, w_out, conv_buf, ret_state, pos):
    B, T, _ = x.shape
    proj = x @ w_in
    splits = np.cumsum([C_CONV, C_CONV, R_QK, R_QK, R_V])
    a, gate_glu, q, k, v, g = jnp.split(proj, splits, axis=-1)
    u = a * jax.nn.sigmoid(gate_glu)
    upad = jnp.concatenate([conv_buf.astype(u.dtype), u], axis=1)
    conv = lax.conv_general_dilated(
        upad, conv_w.astype(u.dtype)[:, None, :], window_strides=(1,), padding='VALID',
        dimension_numbers=('NWC', 'WIO', 'NWC'), feature_group_count=C_CONV) + conv_b
    new_buf = upad[:, -(CONV_K - 1):, :]
    h_conv = jax.nn.silu(layer_norm(conv, cln_g, cln_b))
    qh = rotary(q.reshape(B, T, H_RET, DK_RET), pos)
    kh = rotary(k.reshape(B, T, H_RET, DK_RET), pos) * (DK_RET ** -0.5)
    vh = v.reshape(B, T, H_RET, DV_RET)
    o, new_state = retention(qh.transpose(0, 2, 1, 3), kh.transpose(0, 2, 1, 3),
                             vh.transpose(0, 2, 1, 3), ret_state)
    mu = jnp.mean(o, axis=-1, keepdims=True)
    var = jnp.mean(jnp.square(o - mu), axis=-1, keepdims=True)
    on = ((o - mu) * lax.rsqrt(var + LN_EPS)).transpose(0, 2, 1, 3).reshape(B, T, R_V)
    on = on.astype(x.dtype) * gn_g + gn_b
    h_ret = jax.nn.silu(g) * on
    out = jnp.concatenate([h_conv, h_ret], axis=-1) @ w_out
    return out, new_buf, new_state


def swiglu(x, wg, wu, wd):
    return (jax.nn.silu(x @ wg) * (x @ wu)) @ wd


def moe_swiglu(x, w_router, b_router, wg, wu, wd):
    B, T, D = x.shape
    xf = x.reshape(B * T, D)
    logits = (xf @ w_router).astype(jnp.float32) + b_router.astype(jnp.float32)
    top_val, top_idx = lax.top_k(logits, TOP_K)
    gates = jax.nn.softmax(top_val, axis=-1)
    combine = jnp.sum(jax.nn.one_hot(top_idx, N_EXPERTS, dtype=jnp.float32) * gates[..., None], axis=1)
    combine = combine.astype(x.dtype)
    y = jnp.zeros_like(xf)
    for e in range(N_EXPERTS):
        y = y + combine[:, e:e + 1] * swiglu(xf, wg[e], wu[e], wd[e])
    return y.reshape(B, T, D)


def setup_inputs(seed: int = 0) -> dict:
    key = jax.random.key(seed)
    ks = jax.random.split(key, 32)

    def nrm(k, shape, s):
        return jax.random.normal(k, shape, jnp.float32) * s

    col_scale = jnp.concatenate([
        jnp.full((C_CONV,), BETA, jnp.float32), jnp.ones((C_CONV,), jnp.float32),
        jnp.ones((2 * R_QK,), jnp.float32), jnp.full((R_V,), BETA, jnp.float32),
        jnp.ones((R_V,), jnp.float32)])
    return {
        'x_prompt': nrm(ks[0], (BATCH, SEQ, D_MODEL), 1.0),
        'x_sample': nrm(ks[1], (DEC_BATCH, DEC_SEQ, D_MODEL), 1.0),
        'cache_conv': nrm(ks[2], (DEPTH, DEC_BATCH, CONV_K - 1, C_CONV), 0.3),
        'state_ret': nrm(ks[3], (DEPTH, DEC_BATCH, H_RET, DK_RET, DV_RET), 0.5),
        'w_in': nrm(ks[4], (DEPTH, D_MODEL, D_IN), D_MODEL ** -0.5) * col_scale,
        'conv_w': nrm(ks[5], (DEPTH, CONV_K, C_CONV), CONV_K ** -0.5),
        'conv_b': nrm(ks[6], (DEPTH, C_CONV), 0.02),
        'conv_ln_g': 1.0 + nrm(ks[7], (DEPTH, C_CONV), 0.02),
        'conv_ln_b': nrm(ks[8], (DEPTH, C_CONV), 0.02),
        'ret_gn_g': 1.0 + nrm(ks[9], (DEPTH, R_V), 0.02),
        'ret_gn_b': nrm(ks[10], (DEPTH, R_V), 0.02),
        'w_out': nrm(ks[11], (DEPTH, D_MIX, D_MODEL), BETA * D_MIX ** -0.5),
        'ln1_g': 1.0 + nrm(ks[12], (DEPTH, D_MODEL), 0.02),
        'ln1_b': nrm(ks[13], (DEPTH, D_MODEL), 0.02),
        'ln2_g': 1.0 + nrm(ks[14], (DEPTH, D_MODEL), 0.02),
        'ln2_b': nrm(ks[15], (DEPTH, D_MODEL), 0.02),
        'w_ff_gate': nrm(ks[16], (N_DENSE, D_MODEL, D_FF), D_MODEL ** -0.5),
        'w_ff_up': nrm(ks[17], (N_DENSE, D_MODEL, D_FF), D_MODEL ** -0.5),
        'w_ff_down': nrm(ks[18], (N_DENSE, D_FF, D_MODEL), BETA * D_FF ** -0.5),
        'w_router': nrm(ks[19], (N_MOE, D_MODEL, N_EXPERTS), D_MODEL ** -0.5),
        'b_router': nrm(ks[20], (N_MOE, N_EXPERTS), 0.01),
        'w_exp_gate': nrm(ks[21], (N_MOE, N_EXPERTS, D_MODEL, D_FF_EXPERT), D_MODEL ** -0.5),
        'w_exp_up': nrm(ks[22], (N_MOE, N_EXPERTS, D_MODEL, D_FF_EXPERT), D_MODEL ** -0.5),
        'w_exp_down': nrm(ks[23], (N_MOE, N_EXPERTS, D_FF_EXPERT, D_MODEL), BETA * D_FF_EXPERT ** -0.5),
    }


def reference(x_prompt, x_sample, cache_conv, state_ret, w_in, conv_w, conv_b, conv_ln_g, conv_ln_b,
              ret_gn_g, ret_gn_b, w_out, ln1_g, ln1_b, ln2_g, ln2_b, w_ff_gate, w_ff_up, w_ff_down,
              w_router, b_router, w_exp_gate, w_exp_up, w_exp_down):

    def run_group(x, bufs, states, pos):
        new_bufs, new_states = [], []
        for l in range(DEPTH):
            m, nb, ns = mixer(x, w_in[l], conv_w[l], conv_b[l], conv_ln_g[l], conv_ln_b[l],
                              ret_gn_g[l], ret_gn_b[l], w_out[l], bufs[l], states[l], pos)
            new_bufs.append(nb)
            new_states.append(ns)
            x = layer_norm(ALPHA * x + m, ln1_g[l], ln1_b[l])
            if l % 2 == 0:
                f = swiglu(x, w_ff_gate[l // 2], w_ff_up[l // 2], w_ff_down[l // 2])
            else:
                f = moe_swiglu(x, w_router[l // 2], b_router[l // 2], w_exp_gate[l // 2],
                               w_exp_up[l // 2], w_exp_down[l // 2])
            x = layer_norm(ALPHA * x + f, ln2_g[l], ln2_b[l])
        return x, jnp.stack(new_bufs), jnp.stack(new_states)

    zero_buf = jnp.zeros((x_prompt.shape[0], CONV_K - 1, C_CONV), x_prompt.dtype)
    zero_state = jnp.zeros((x_prompt.shape[0], H_RET, DK_RET, DV_RET), jnp.float32)
    pos_prompt = jnp.arange(x_prompt.shape[1], dtype=jnp.int32)
    y_prompt, new_conv_prompt, new_ret_prompt = run_group(
        x_prompt, [zero_buf] * DEPTH, [zero_state] * DEPTH, pos_prompt)

    pos_sample = PAST_LEN + jnp.arange(x_sample.shape[1], dtype=jnp.int32)
    y_sample, new_conv_sample, new_ret_sample = run_group(x_sample, cache_conv, state_ret, pos_sample)

    return (y_prompt, y_sample, new_conv_prompt, new_ret_prompt, new_conv_sample, new_ret_sample)
```

```python
import functools

import numpy as np
import jax
import jax.numpy as jnp
from jax import lax
from jax.experimental import pallas as pl
from jax.experimental.pallas import tpu as pltpu

F32 = jnp.float32
BF16 = jnp.bfloat16

LN_EPS = 1e-5
ROPE_BASE = 10000.0
RET_CHUNK = 128
PAST_LEN = 16384
TOP_K = 2

V7X_LANES = 128
V7X_BF16_SUBLANES = 16
V7X_VMEM_LIMIT_BYTES = 56 * 1024 * 1024

HIST_ROWS = 32
MOE_TILE = 512


def _cparams(n_axes):
    return pltpu.CompilerParams(dimension_semantics=("arbitrary",) * n_axes,
                                vmem_limit_bytes=V7X_VMEM_LIMIT_BYTES)


def _tile(n, pref, mult=8):
    t = min(pref, n)
    while t > mult and (n % t or t % mult):
        t -= mult
    assert n % t == 0 and t % mult == 0, (n, pref, mult)
    return t


def _layer_norm(y, g, b):
    mu = jnp.mean(y, axis=-1, keepdims=True)
    d = y - mu
    var = jnp.mean(d * d, axis=-1, keepdims=True)
    return d * lax.rsqrt(var + LN_EPS) * g + b


def _silu(z):
    return z * jax.nn.sigmoid(z)


def _cast_kernel(w_ref, o_ref):
    o_ref[...] = w_ref[...].astype(o_ref.dtype)


def cast_weight(w, l):
    _, K, N = w.shape
    tr = _tile(K, 512, V7X_BF16_SUBLANES)
    return pl.pallas_call(
        _cast_kernel, grid=(K // tr,),
        in_specs=[pl.BlockSpec((None, tr, N), lambda i: (l, i, 0))],
        out_specs=pl.BlockSpec((tr, N), lambda i: (i, 0)),
        out_shape=jax.ShapeDtypeStruct((K, N), BF16),
        compiler_params=_cparams(1), name="cast_weight")(w)


def _matmul_ws_kernel(x_ref, w_ref, o_ref, wb_ref):
    @pl.when(pl.program_id(1) == 0)
    def _():
        wb_ref[...] = w_ref[...].astype(BF16)
    o_ref[...] = jnp.dot(x_ref[...], wb_ref[...], preferred_element_type=F32).astype(o_ref.dtype)


def matmul_ws(x, w, l, out_dtype):
    M, K = x.shape
    N = w.shape[-1]
    tm = _tile(M, 1024, V7X_BF16_SUBLANES)
    tn = _tile(N, 512, V7X_LANES)
    return pl.pallas_call(
        _matmul_ws_kernel, grid=(N // tn, M // tm),
        in_specs=[pl.BlockSpec((tm, K), lambda j, i: (i, 0)),
                  pl.BlockSpec((None, K, tn), lambda j, i: (l, 0, j))],
        out_specs=pl.BlockSpec((tm, tn), lambda j, i: (i, j)),
        out_shape=jax.ShapeDtypeStruct((M, N), out_dtype),
        scratch_shapes=[pltpu.VMEM((K, tn), BF16)],
        compiler_params=_cparams(2), name="in_proj")(x, w)


def _conv_post(conv_chunks, lg_ref, lb_ref, o_ref, C):
    s = conv_chunks[0]
    for ch in conv_chunks[1:]:
        s = s + ch
    mu = jnp.sum(s, axis=-1, keepdims=True) / C
    sq = None
    for ch in conv_chunks:
        d = ch - mu
        sq = d * d if sq is None else sq + d * d
    var = jnp.sum(sq, axis=-1, keepdims=True) / C
    inv = lax.rsqrt(var + LN_EPS)
    for c, ch in enumerate(conv_chunks):
        sl = slice(c * V7X_LANES, (c + 1) * V7X_LANES)
        z = (ch - mu) * inv * lg_ref[:, sl] + lb_ref[:, sl]
        o_ref[:, sl] = _silu(z).astype(o_ref.dtype)


def _conv_prompt_kernel(a_ref, g_ref, ha_ref, hg_ref, w3_ref, cb3_ref, lg_ref, lb_ref,
                        o_ref, nb_ref, upad_ref, conv_ref, *, tt, K, C, rb):
    i = pl.program_id(1)
    nC = C // V7X_LANES
    off = HIST_ROWS - (K - 1)
    u = a_ref[...] * jax.nn.sigmoid(g_ref[...])
    uh = ha_ref[...] * jax.nn.sigmoid(hg_ref[...])
    uh = jnp.where(i > 0, uh, 0.0)
    for c in range(nC):
        sl = slice(c * V7X_LANES, (c + 1) * V7X_LANES)
        upad_ref[c, 0:HIST_ROWS, :] = uh[:, sl]
        upad_ref[c, HIST_ROWS:HIST_ROWS + tt, :] = u[:, sl]

    def chunk_body(c, carry):
        for r in range(tt // rb):
            acc = jnp.zeros((rb, V7X_LANES), F32)
            for j in range(K):
                lo = r * rb + off + j
                acc = acc + upad_ref[c, lo:lo + rb, :] * w3_ref[c, j:j + 1, :]
            conv_ref[c, r * rb:(r + 1) * rb, :] = acc + cb3_ref[c]
        return carry

    lax.fori_loop(0, nC, chunk_body, 0)
    _conv_post([conv_ref[c] for c in range(nC)], lg_ref, lb_ref, o_ref, C)

    @pl.when(i == pl.num_programs(1) - 1)
    def _():
        for c in range(nC):
            sl = slice(c * V7X_LANES, (c + 1) * V7X_LANES)
            nb_ref[:, sl] = upad_ref[c, HIST_ROWS + tt - (K - 1):HIST_ROWS + tt, :]


def conv_prompt(proj, w3, cb3, lg, lb, B, T, C, K):
    tt = _tile(T, 256, HIST_ROWS)
    nT = T // tt
    nC = C // V7X_LANES
    hb = tt // HIST_ROWS
    kern = functools.partial(_conv_prompt_kernel, tt=tt, K=K, C=C, rb=_tile(tt, 64))
    return pl.pallas_call(
        kern, grid=(B, nT),
        in_specs=[pl.BlockSpec((tt, C), lambda b, i: (b * nT + i, 0)),
                  pl.BlockSpec((tt, C), lambda b, i: (b * nT + i, 1)),
                  pl.BlockSpec((HIST_ROWS, C), lambda b, i: (jnp.maximum((b * nT + i) * hb - 1, 0), 0)),
                  pl.BlockSpec((HIST_ROWS, C), lambda b, i: (jnp.maximum((b * nT + i) * hb - 1, 0), 1)),
                  pl.BlockSpec((nC, K, V7X_LANES), lambda b, i: (0, 0, 0)),
                  pl.BlockSpec((nC, 1, V7X_LANES), lambda b, i: (0, 0, 0)),
                  pl.BlockSpec((1, C), lambda b, i: (0, 0)),
                  pl.BlockSpec((1, C), lambda b, i: (0, 0))],
        out_specs=[pl.BlockSpec((tt, C), lambda b, i: (b * nT + i, 0)),
                   pl.BlockSpec((None, K - 1, C), lambda b, i: (b, 0, 0))],
        out_shape=[jax.ShapeDtypeStruct((B * T, C), BF16),
                   jax.ShapeDtypeStruct((B, K - 1, C), F32)],
        scratch_shapes=[pltpu.VMEM((nC, HIST_ROWS + tt, V7X_LANES), F32),
                        pltpu.VMEM((nC, tt, V7X_LANES), F32)],
        compiler_params=_cparams(2), name="conv_prompt")(proj, proj, proj, proj, w3, cb3, lg, lb)


def _conv_sample_kernel(a_ref, g_ref, cache_ref, w_ref, cb_ref, lg_ref, lb_ref,
                        o_ref, nb_ref, upad_ref, *, T, K, C, bb):
    nC = C // V7X_LANES
    u = a_ref[...] * jax.nn.sigmoid(g_ref[...])
    upad_ref[:, 0:K - 1, :] = cache_ref[...]
    upad_ref[:, K - 1:K - 1 + T, :] = u.reshape(bb, T, C)
    nb_ref[...] = upad_ref[:, T:T + K - 1, :]
    chunks = []
    for c in range(nC):
        sl = slice(c * V7X_LANES, (c + 1) * V7X_LANES)
        acc = jnp.zeros((bb, T, V7X_LANES), F32)
        for j in range(K):
            acc = acc + upad_ref[:, j:j + T, sl] * w_ref[j:j + 1, sl]
        chunks.append(acc.reshape(bb * T, V7X_LANES) + cb_ref[:, sl])
    _conv_post(chunks, lg_ref, lb_ref, o_ref, C)


def conv_sample(proj, cache, l, w, cb, lg, lb, row0, B, T, C, K):
    assert T % 8 == 0
    bb = _tile(B, 16, 2)
    assert row0 % (bb * T) == 0
    blk0 = row0 // (bb * T)
    kern = functools.partial(_conv_sample_kernel, T=T, K=K, C=C, bb=bb)
    return pl.pallas_call(
        kern, grid=(B // bb,),
        in_specs=[pl.BlockSpec((bb * T, C), lambda s: (blk0 + s, 0)),
                  pl.BlockSpec((bb * T, C), lambda s: (blk0 + s, 1)),
                  pl.BlockSpec((None, bb, K - 1, C), lambda s: (l, s, 0, 0)),
                  pl.BlockSpec((K, C), lambda s: (0, 0)),
                  pl.BlockSpec((1, C), lambda s: (0, 0)),
                  pl.BlockSpec((1, C), lambda s: (0, 0)),
                  pl.BlockSpec((1, C), lambda s: (0, 0))],
        out_specs=[pl.BlockSpec((bb * T, C), lambda s: (s, 0)),
                   pl.BlockSpec((bb, K - 1, C), lambda s: (s, 0, 0))],
        out_shape=[jax.ShapeDtypeStruct((B * T, C), BF16),
                   jax.ShapeDtypeStruct((B, K - 1, C), F32)],
        scratch_shapes=[pltpu.VMEM((bb, K - 1 + T, C), F32)],
        compiler_params=_cparams(1), name="conv_sample")(proj, proj, cache, w, cb, lg, lb)


def _retention_tables(H, chunk, rows):
    gamma = (1.0 - 2.0 ** (-5.0 - np.arange(H, dtype=np.float32))).astype(np.float32)
    log_g = np.log(gamma).astype(np.float32)
    idx = np.arange(chunk, dtype=np.float32)
    diff = idx[:, None] - idx[None, :]
    dmask = np.where(diff[None] >= 0, np.exp(diff[None] * log_g[:, None, None]), 0.0).astype(np.float32)
    q_dec = np.exp((idx[None, :] + np.float32(1.0)) * log_g[:, None]).astype(np.float32)
    k_dec = np.exp((np.float32(chunk) - np.float32(1.0) - idx[None, :]) * log_g[:, None]).astype(np.float32)
    c_dec = np.exp(np.float32(chunk) * log_g).astype(np.float32)
    dm = np.zeros((H, rows, rows), np.float32)
    dm[:, :chunk, :chunk] = dmask
    qd = np.zeros((H, rows, 1), np.float32)
    qd[:, :chunk, 0] = q_dec
    kd = np.zeros((H, rows, 1), np.float32)
    kd[:, :chunk, 0] = k_dec
    return dm, qd, kd, [float(c) for c in c_dec]


def _rotary_tables(pos, half):
    inv = (1.0 / (ROPE_BASE ** (np.arange(half, dtype=np.float32) / half))).astype(np.float32)
    ang = (pos.astype(np.float32)[:, None] * inv[None, :]).astype(np.float32)
    return np.cos(ang).astype(np.float32), np.sin(ang).astype(np.float32)


def _retention_kernel(*refs, H, DK, C, nb, rows, c_dec, has_s0):
    if has_s0:
        (q_ref, k_ref, v_ref, g_ref, cos_ref, sin_ref, dm_ref, qd_ref, kd_ref, gng_ref, gnb_ref,
         s0_ref, o_ref, ns_ref) = refs
    else:
        (q_ref, k_ref, v_ref, g_ref, cos_ref, sin_ref, dm_ref, qd_ref, kd_ref, gng_ref, gnb_ref,
         o_ref, ns_ref) = refs
        s0_ref = None
    half = DK // 2
    scale = DK ** -0.5

    @pl.when(pl.program_id(1) == 0)
    def _():
        if has_s0:
            ns_ref[...] = s0_ref[...]
        else:
            ns_ref[...] = jnp.zeros(ns_ref.shape, F32)

    cos = cos_ref[...]
    sin = sin_ref[...]

    def rot(x):
        x1, x2 = x[:, :half], x[:, half:]
        return jnp.concatenate([x1 * cos - x2 * sin, x1 * sin + x2 * cos], axis=-1)

    def pad_rows(x):
        if rows == C:
            return x
        return jnp.concatenate([x, jnp.zeros((rows - C, x.shape[1]), x.dtype)], axis=0)

    for s in range(nb):
        rs = slice(s * C, (s + 1) * C)
        for h in range(H):
            hs = slice(h * DK, (h + 1) * DK)
            qr = pad_rows(rot(q_ref[rs, hs]))
            kr = pad_rows(rot(k_ref[rs, hs]) * scale)
            vb = pad_rows(v_ref[rs, hs]).astype(BF16)
            S = ns_ref[s, h]
            scores = lax.dot_general(qr.astype(BF16), kr.astype(BF16), (((1,), (1,)), ((), ())),
                                     preferred_element_type=F32) * dm_ref[h]
            o = (jnp.dot(scores.astype(BF16), vb, preferred_element_type=F32)
                 + jnp.dot((qr * qd_ref[h]).astype(BF16), S.astype(BF16), preferred_element_type=F32))
            kd = (kr * kd_ref[h]).astype(BF16)
            ns_ref[s, h] = c_dec[h] * S + lax.dot_general(kd, vb, (((0,), (0,)), ((), ())),
                                                          preferred_element_type=F32)
            o = o[:C]
            mu = jnp.mean(o, axis=-1, keepdims=True)
            d = o - mu
            var = jnp.mean(d * d, axis=-1, keepdims=True)
            on = d * lax.rsqrt(var + LN_EPS) * gng_ref[:, hs] + gnb_ref[:, hs]
            o_ref[rs, hs] = (_silu(g_ref[rs, hs]) * on).astype(o_ref.dtype)


def retention(proj, gng, gnb, row0, B, T, H, DK, pos0, state=None, l=0):
    R = H * DK
    C = int(np.gcd(T, RET_CHUNK))
    n_chunks = T // C
    nb = 1 if n_chunks > 1 else _tile(B, 4, 1)
    rows = max(C, V7X_BF16_SUBLANES)
    dm, qd, kd, c_dec = _retention_tables(H, C, rows)
    cos, sin = _rotary_tables(pos0 + np.arange(T), DK // 2)
    blk_rows = nb * C
    assert row0 % blk_rows == 0 and (blk_rows % V7X_BF16_SUBLANES == 0)
    blk0 = row0 // blk_rows
    has_s0 = state is not None
    kern = functools.partial(_retention_kernel, H=H, DK=DK, C=C, nb=nb, rows=rows,
                             c_dec=c_dec, has_s0=has_s0)

    def row_map(col):
        return lambda sb, c: (blk0 + sb * n_chunks + c, col)

    in_specs = [pl.BlockSpec((blk_rows, R), row_map(2)),
                pl.BlockSpec((blk_rows, R), row_map(3)),
                pl.BlockSpec((blk_rows, R), row_map(4)),
                pl.BlockSpec((blk_rows, R), row_map(5)),
                pl.BlockSpec((C, DK // 2), lambda sb, c: (c, 0)),
                pl.BlockSpec((C, DK // 2), lambda sb, c: (c, 0)),
                pl.BlockSpec((H, rows, rows), lambda sb, c: (0, 0, 0)),
                pl.BlockSpec((H, rows, 1), lambda sb, c: (0, 0, 0)),
                pl.BlockSpec((H, rows, 1), lambda sb, c: (0, 0, 0)),
                pl.BlockSpec((1, R), lambda sb, c: (0, 0)),
                pl.BlockSpec((1, R), lambda sb, c: (0, 0))]
    args = [proj, proj, proj, proj, jnp.asarray(cos), jnp.asarray(sin), jnp.asarray(dm),
            jnp.asarray(qd), jnp.asarray(kd), gng, gnb]
    if has_s0:
        in_specs.append(pl.BlockSpec((None, nb, H, DK, DK), lambda sb, c: (l, sb, 0, 0, 0)))
        args.append(state)
    return pl.pallas_call(
        kern, grid=(B // nb, n_chunks),
        in_specs=in_specs,
        out_specs=[pl.BlockSpec((blk_rows, R), lambda sb, c: (sb * n_chunks + c, 0)),
                   pl.BlockSpec((nb, H, DK, DK), lambda sb, c: (sb, 0, 0, 0))],
        out_shape=[jax.ShapeDtypeStruct((B * T, R), BF16),
                   jax.ShapeDtypeStruct((B, H, DK, DK), F32)],
        compiler_params=_cparams(2), name="retention")(*args)


def _outproj_ln_kernel(hc_ref, hr_ref, x_ref, w_ref, g_ref, b_ref, xo_ref, xb_ref, *, alpha, cc):
    m = (jnp.dot(hc_ref[...], w_ref[0:cc, :], preferred_element_type=F32)
         + jnp.dot(hr_ref[...], w_ref[cc:, :], preferred_element_type=F32))
    z = _layer_norm(alpha * x_ref[...] + m, g_ref[...], b_ref[...])
    xo_ref[...] = z
    xb_ref[...] = z.astype(BF16)


def outproj_ln(hc, hr, x, wb, g, b, alpha):
    M, D = x.shape
    cc = hc.shape[1]
    tm = _tile(M, 512, V7X_BF16_SUBLANES)
    kern = functools.partial(_outproj_ln_kernel, alpha=alpha, cc=cc)
    return pl.pallas_call(
        kern, grid=(M // tm,),
        in_specs=[pl.BlockSpec((tm, cc), lambda i: (i, 0)),
                  pl.BlockSpec((tm, hr.shape[1]), lambda i: (i, 0)),
                  pl.BlockSpec((tm, D), lambda i: (i, 0)),
                  pl.BlockSpec(wb.shape, lambda i: (0, 0)),
                  pl.BlockSpec((1, D), lambda i: (0, 0)),
                  pl.BlockSpec((1, D), lambda i: (0, 0))],
        out_specs=[pl.BlockSpec((tm, D), lambda i: (i, 0)),
                   pl.BlockSpec((tm, D), lambda i: (i, 0))],
        out_shape=[jax.ShapeDtypeStruct((M, D), F32), jax.ShapeDtypeStruct((M, D), BF16)],
        compiler_params=_cparams(1), name="outproj_ln")(hc, hr, x, wb, g, b)


def _resid_ln_kernel(f_ref, x_ref, g_ref, b_ref, xo_ref, xb_ref, *, alpha):
    z = _layer_norm(alpha * x_ref[...] + f_ref[...], g_ref[...], b_ref[...])
    xo_ref[...] = z
    xb_ref[...] = z.astype(BF16)


def resid_ln(f, x, g, b, alpha):
    M, D = x.shape
    tm = _tile(M, 512, V7X_BF16_SUBLANES)
    kern = functools.partial(_resid_ln_kernel, alpha=alpha)
    row = pl.BlockSpec((tm, D), lambda i: (i, 0))
    vec = pl.BlockSpec((1, D), lambda i: (0, 0))
    return pl.pallas_call(
        kern, grid=(M // tm,), in_specs=[row, row, vec, vec], out_specs=[row, row],
        out_shape=[jax.ShapeDtypeStruct((M, D), F32), jax.ShapeDtypeStruct((M, D), BF16)],
        compiler_params=_cparams(1), name="resid_ln")(f, x, g, b)


def _tile_state(te_ref, na_ref):
    i = pl.program_id(1)
    active = i < na_ref[0]
    changed = jnp.logical_or(i == 0, te_ref[i] != te_ref[jnp.maximum(i - 1, 0)])
    return active, changed


def _gateup_kernel(te_ref, na_ref, x_ref, wg_ref, wu_ref, o_ref, wgb_ref, wub_ref):
    active, changed = _tile_state(te_ref, na_ref)

    @pl.when(jnp.logical_and(active, changed))
    def _():
        wgb_ref[...] = wg_ref[...].astype(BF16)
        wub_ref[...] = wu_ref[...].astype(BF16)

    @pl.when(active)
    def _():
        x = x_ref[...]
        gate = jnp.dot(x, wgb_ref[...], preferred_element_type=F32)
        up = jnp.dot(x, wub_ref[...], preferred_element_type=F32)
        o_ref[...] = (_silu(gate) * up).astype(o_ref.dtype)

    @pl.when(jnp.logical_not(active))
    def _():
        o_ref[...] = jnp.zeros(o_ref.shape, o_ref.dtype)


def _down_kernel(te_ref, na_ref, h_ref, w_ref, o_ref, wb_ref):
    active, changed = _tile_state(te_ref, na_ref)

    @pl.when(jnp.logical_and(active, changed))
    def _():
        wb_ref[...] = w_ref[...].astype(BF16)

    @pl.when(active)
    def _():
        o_ref[...] = jnp.dot(h_ref[...], wb_ref[...], preferred_element_type=F32)

    @pl.when(jnp.logical_not(active))
    def _():
        o_ref[...] = jnp.zeros(o_ref.shape, o_ref.dtype)


def grouped_swiglu(xs, wg, wu, wd, li, tile_expert, n_active, tm):
    S, D = xs.shape
    Fd = wg.shape[-1]
    nT = S // tm
    tn = _tile(Fd, 512, V7X_LANES)

    def row_map(j, i, te, na):
        return (jnp.minimum(i, na[0] - 1), 0)

    h = pl.pallas_call(
        _gateup_kernel,
        grid_spec=pltpu.PrefetchScalarGridSpec(
            num_scalar_prefetch=2, grid=(Fd // tn, nT),
            in_specs=[pl.BlockSpec((tm, D), row_map),
                      pl.BlockSpec((None, None, D, tn), lambda j, i, te, na: (li, te[i], 0, j)),
                      pl.BlockSpec((None, None, D, tn), lambda j, i, te, na: (li, te[i], 0, j))],
            out_specs=pl.BlockSpec((tm, tn), lambda j, i, te, na: (i, j)),
            scratch_shapes=[pltpu.VMEM((D, tn), BF16), pltpu.VMEM((D, tn), BF16)]),
        out_shape=jax.ShapeDtypeStruct((S, Fd), BF16),
        compiler_params=_cparams(2), name="ffn_gate_up")(tile_expert, n_active, xs, wg, wu)

    tn2 = _tile(D, 512, V7X_LANES)
    return pl.pallas_call(
        _down_kernel,
        grid_spec=pltpu.PrefetchScalarGridSpec(
            num_scalar_prefetch=2, grid=(D // tn2, nT),
            in_specs=[pl.BlockSpec((tm, Fd), row_map),
                      pl.BlockSpec((None, None, Fd, tn2), lambda j, i, te, na: (li, te[i], 0, j))],
            out_specs=pl.BlockSpec((tm, tn2), lambda j, i, te, na: (i, j)),
            scratch_shapes=[pltpu.VMEM((Fd, tn2), BF16)]),
        out_shape=jax.ShapeDtypeStruct((S, D), F32),
        compiler_params=_cparams(2), name="ffn_down")(tile_expert, n_active, h, wd)


def _router_kernel(x_ref, w_ref, b_ref, idx_ref, gate_ref):
    logits = jnp.dot(x_ref[...], w_ref[...], precision=lax.Precision.HIGHEST,
                     preferred_element_type=F32) + b_ref[...]
    col = lax.broadcasted_iota(jnp.int32, logits.shape, 1)
    big = jnp.int32(logits.shape[1])
    m1 = jnp.max(logits, axis=-1, keepdims=True)
    i1 = jnp.min(jnp.where(logits == m1, col, big), axis=-1, keepdims=True)
    rest = jnp.where(col == i1, -jnp.inf, logits)
    m2 = jnp.max(rest, axis=-1, keepdims=True)
    i2 = jnp.min(jnp.where(rest == m2, col, big), axis=-1, keepdims=True)
    e = jnp.exp(m2 - m1)
    g1 = 1.0 / (1.0 + e)
    g2 = e / (1.0 + e)
    idx_ref[...] = jnp.where(col == 0, i1, jnp.where(col == 1, i2, 0))
    gate_ref[...] = jnp.where(col == 0, g1, jnp.where(col == 1, g2, 0.0))


def router(x, w_pad, b_pad):
    M, D = x.shape
    tm = _tile(M, 512)
    out = pl.BlockSpec((tm, V7X_LANES), lambda i: (i, 0))
    return pl.pallas_call(
        _router_kernel, grid=(M // tm,),
        in_specs=[pl.BlockSpec((tm, D), lambda i: (i, 0)),
                  pl.BlockSpec((D, V7X_LANES), lambda i: (0, 0)),
                  pl.BlockSpec((1, V7X_LANES), lambda i: (0, 0))],
        out_specs=[out, out],
        out_shape=[jax.ShapeDtypeStruct((M, V7X_LANES), jnp.int32),
                   jax.ShapeDtypeStruct((M, V7X_LANES), F32)],
        compiler_params=_cparams(1), name="router")(x, w_pad, b_pad)


def _dispatch_kernel(src_ref, na_ref, x_hbm, o_ref, buf_ref, sem, *, R):
    i = pl.program_id(0)

    @pl.when(i < na_ref[0])
    def _():
        def issue(r, carry):
            tok = src_ref[i * R + r]
            pltpu.make_async_copy(x_hbm.at[pl.ds(tok, 1), :], buf_ref.at[pl.ds(r, 1), :], sem).start()
            return carry
        lax.fori_loop(0, R, issue, 0)
        pltpu.make_async_copy(x_hbm.at[pl.ds(0, R), :], buf_ref, sem).wait()
        o_ref[...] = buf_ref[...].astype(o_ref.dtype)

    @pl.when(i >= na_ref[0])
    def _():
        o_ref[...] = jnp.zeros(o_ref.shape, o_ref.dtype)


def dispatch(x, src, n_active, S, R):
    M, D = x.shape
    kern = functools.partial(_dispatch_kernel, R=R)
    return pl.pallas_call(
        kern,
        grid_spec=pltpu.PrefetchScalarGridSpec(
            num_scalar_prefetch=2, grid=(S // R,),
            in_specs=[pl.BlockSpec(memory_space=pl.ANY)],
            out_specs=pl.BlockSpec((R, D), lambda i, src, na: (i, 0)),
            scratch_shapes=[pltpu.VMEM((R, D), F32), pltpu.SemaphoreType.DMA(())]),
        out_shape=jax.ShapeDtypeStruct((S, D), BF16),
        compiler_params=_cparams(1), name="moe_dispatch")(src, n_active, x)


def _combine_ln_kernel(pos_ref, ys_hbm, x_ref, gate_ref, g_ref, b_ref, xo_ref, xb_ref, buf_ref, sem,
                       *, R, alpha):
    i = pl.program_id(0)

    def issue(r, carry):
        a = (i * R + r) * TOP_K
        for k in range(TOP_K):
            pltpu.make_async_copy(ys_hbm.at[pl.ds(pos_ref[a + k], 1), :],
                                  buf_ref.at[k, pl.ds(r, 1), :], sem).start()
        return carry
    lax.fori_loop(0, R, issue, 0)
    for k in range(TOP_K):
        pltpu.make_async_copy(ys_hbm.at[pl.ds(0, R), :], buf_ref.at[k], sem).wait()
    gates = gate_ref[...]
    y = gates[:, 0:1] * buf_ref[0]
    for k in range(1, TOP_K):
        y = y + gates[:, k:k + 1] * buf_ref[k]
    z = _layer_norm(alpha * x_ref[...] + y, g_ref[...], b_ref[...])
    xo_ref[...] = z
    xb_ref[...] = z.astype(BF16)


def combine_ln(ys, pos, x, gates, g, b, alpha):
    M, D = x.shape
    R = _tile(M, 256, V7X_BF16_SUBLANES)
    kern = functools.partial(_combine_ln_kernel, R=R, alpha=alpha)
    row = lambda i, pos: (i, 0)
    fix = lambda i, pos: (0, 0)
    return pl.pallas_call(
        kern,
        grid_spec=pltpu.PrefetchScalarGridSpec(
            num_scalar_prefetch=1, grid=(M // R,),
            in_specs=[pl.BlockSpec(memory_space=pl.ANY),
                      pl.BlockSpec((R, D), row),
                      pl.BlockSpec((R, V7X_LANES), row),
                      pl.BlockSpec((1, D), fix),
                      pl.BlockSpec((1, D), fix)],
            out_specs=[pl.BlockSpec((R, D), row), pl.BlockSpec((R, D), row)],
            scratch_shapes=[pltpu.VMEM((TOP_K, R, D), F32), pltpu.SemaphoreType.DMA(())]),
        out_shape=[jax.ShapeDtypeStruct((M, D), F32), jax.ShapeDtypeStruct((M, D), BF16)],
        compiler_params=_cparams(1), name="moe_combine_ln")(pos, ys, x, gates, g, b)


def _routing_plan(idx, E, tm, n_tiles):
    e_flat = idx.reshape(-1)
    A = e_flat.shape[0]
    onehot = (e_flat[:, None] == jnp.arange(E, dtype=jnp.int32)[None, :]).astype(jnp.int32)
    csum = jnp.cumsum(onehot, axis=0)
    counts = csum[-1]
    rank = jnp.take_along_axis(csum, e_flat[:, None], axis=1)[:, 0] - 1
    tiles_e = (counts + tm - 1) // tm
    tile_end = jnp.cumsum(tiles_e)
    pad_start = (tile_end - tiles_e) * tm
    start = jnp.cumsum(counts) - counts
    n_active = tile_end[-1]
    pos = pad_start[e_flat] + rank
    tile_id = jnp.arange(n_tiles, dtype=jnp.int32)
    te = jnp.searchsorted(tile_end, jnp.minimum(tile_id, n_active - 1), side="right").astype(jnp.int32)
    order = jnp.argsort(e_flat, stable=True).astype(jnp.int32)
    slot = jnp.arange(n_tiles * tm, dtype=jnp.int32)
    se = te[slot // tm]
    off = slot - pad_start[se]
    r = start[se] + jnp.clip(off, 0, jnp.maximum(counts[se] - 1, 0))
    src = order[jnp.clip(r, 0, A - 1)] // TOP_K
    return src.astype(jnp.int32), pos.astype(jnp.int32), te, n_active.reshape(1).astype(jnp.int32)


def kernel(x_prompt, x_sample, cache_conv, state_ret, w_in, conv_w, conv_b, conv_ln_g, conv_ln_b,
           ret_gn_g, ret_gn_b, w_out, ln1_g, ln1_b, ln2_g, ln2_b, w_ff_gate, w_ff_up, w_ff_down,
           w_router, b_router, w_exp_gate, w_exp_up, w_exp_down):
    B, T, D = x_prompt.shape
    Bs, Ts, _ = x_sample.shape
    L = w_in.shape[0]
    K = conv_w.shape[1]
    C = conv_w.shape[2]
    H, DK = state_ret.shape[2], state_ret.shape[3]
    E = w_router.shape[-1]
    Np, Ns = B * T, Bs * Ts
    M = Np + Ns
    nC = C // V7X_LANES
    alpha = float((2 * L) ** 0.25)
    assert w_in.shape[-1] == 6 * C and H * DK == C and D == 2 * C

    x = jnp.concatenate([x_prompt.reshape(Np, D), x_sample.reshape(Ns, D)], axis=0)
    xb = x.astype(BF16)

    tm_dense = _tile(M, 1024, V7X_BF16_SUBLANES)
    n_dense_tiles = M // tm_dense
    moe_tm = _tile(M * TOP_K, MOE_TILE, V7X_BF16_SUBLANES)
    moe_tiles = (M * TOP_K) // moe_tm + E
    w_ff_gate4, w_ff_up4, w_ff_down4 = w_ff_gate[:, None], w_ff_up[:, None], w_ff_down[:, None]

    new_conv_p, new_ret_p, new_conv_s, new_ret_s = [], [], [], []
    for l in range(L):
        proj = matmul_ws(xb, w_in, l, F32)

        w3 = conv_w[l].reshape(K, nC, V7X_LANES).transpose(1, 0, 2)
        cb3 = conv_b[l].reshape(nC, 1, V7X_LANES)
        cb, lg, lb = conv_b[l][None], conv_ln_g[l][None], conv_ln_b[l][None]
        hc_p, nbuf_p = conv_prompt(proj, w3, cb3, lg, lb, B, T, C, K)
        hc_s, nbuf_s = conv_sample(proj, cache_conv, l, conv_w[l], cb, lg, lb, Np, Bs, Ts, C, K)

        gng, gnb = ret_gn_g[l][None], ret_gn_b[l][None]
        hr_p, nst_p = retention(proj, gng, gnb, 0, B, T, H, DK, 0)
        hr_s, nst_s = retention(proj, gng, gnb, Np, Bs, Ts, H, DK, PAST_LEN, state=state_ret, l=l)
        new_conv_p.append(nbuf_p)
        new_conv_s.append(nbuf_s)
        new_ret_p.append(nst_p)
        new_ret_s.append(nst_s)

        hc = jnp.concatenate([hc_p, hc_s], axis=0)
        hr = jnp.concatenate([hr_p, hr_s], axis=0)
        x, xb = outproj_ln(hc, hr, x, cast_weight(w_out, l), ln1_g[l][None], ln1_b[l][None], alpha)

        if l % 2 == 0:
            te = jnp.zeros((n_dense_tiles,), jnp.int32)
            na = jnp.full((1,), n_dense_tiles, jnp.int32)
            f = grouped_swiglu(xb, w_ff_gate4, w_ff_up4, w_ff_down4, l // 2, te, na, tm_dense)
            x, xb = resid_ln(f, x, ln2_g[l][None], ln2_b[l][None], alpha)
        else:
            li = l // 2
            w_pad = jnp.zeros((D, V7X_LANES), F32).at[:, :E].set(w_router[li])
            b_pad = jnp.full((1, V7X_LANES), -1e30, F32).at[0, :E].set(b_router[li])
            idx_w, gate_w = router(x, w_pad, b_pad)
            src, pos, te, na = _routing_plan(idx_w[:, :TOP_K], E, moe_tm, moe_tiles)
            xs = dispatch(x, src, na, moe_tiles * moe_tm, moe_tm)
            ys = grouped_swiglu(xs, w_exp_gate, w_exp_up, w_exp_down, li, te, na, moe_tm)
            x, xb = combine_ln(ys, pos, x, gate_w, ln2_g[l][None], ln2_b[l][None], alpha)

    y_prompt = x[:Np].reshape(B, T, D)
    y_sample = x[Np:].reshape(Bs, Ts, D)
    return (y_prompt, y_sample, jnp.stack(new_conv_p), jnp.stack(new_ret_p),
            jnp.stack(new_conv_s), jnp.stack(new_ret_s))
```

```python
import functools

import numpy as np
import jax
import jax.numpy as jnp
from jax import lax
from jax.experimental import pallas as pl
from jax.experimental.pallas import tpu as pltpu

F32 = jnp.float32
BF16 = jnp.bfloat16

LN_EPS = 1e-5
ROPE_BASE = 10000.0
RET_CHUNK = 128
PAST_LEN = 16384
TOP_K = 2

V7X_LANES = 128
V7X_BF16_SUBLANES = 16
V7X_VMEM_LIMIT_BYTES = 56 * 1024 * 1024

HIST_ROWS = 32
MOE_TILE = 512
ANY = pl.BlockSpec(memory_space=pl.ANY)


def _cparams(n_axes):
    return pltpu.CompilerParams(dimension_semantics=("arbitrary",) * n_axes,
                                vmem_limit_bytes=V7X_VMEM_LIMIT_BYTES)


def _tile(n, pref, mult=8):
    t = min(pref, n)
    while t > mult and (n % t or t % mult):
        t -= mult
    assert n % t == 0 and t % mult == 0, (n, pref, mult)
    return t


def _skip_refs(kern, start, count):
    def wrapped(*refs):
        return kern(*refs[:start], *refs[start + count:])
    return wrapped


def _layer_norm(y, g, b):
    mu = jnp.mean(y, axis=-1, keepdims=True)
    d = y - mu
    var = jnp.mean(d * d, axis=-1, keepdims=True)
    return d * lax.rsqrt(var + LN_EPS) * g + b


def _silu(z):
    return z * jax.nn.sigmoid(z)


def _cast_kernel(w_ref, o_ref):
    o_ref[...] = w_ref[...].astype(o_ref.dtype)


def cast_weight(w, l):
    _, K, N = w.shape
    tr = _tile(K, 512, V7X_BF16_SUBLANES)
    return pl.pallas_call(
        _cast_kernel, grid=(K // tr,),
        in_specs=[pl.BlockSpec((None, tr, N), lambda i: (l, i, 0))],
        out_specs=pl.BlockSpec((tr, N), lambda i: (i, 0)),
        out_shape=jax.ShapeDtypeStruct((K, N), BF16),
        compiler_params=_cparams(1), name="cast_weight")(w)


def _matmul_ws_kernel(x_ref, w_ref, o_ref, wb_ref):
    @pl.when(pl.program_id(1) == 0)
    def _():
        wb_ref[...] = w_ref[...].astype(BF16)
    o_ref[...] = jnp.dot(x_ref[...], wb_ref[...], preferred_element_type=F32).astype(o_ref.dtype)


def matmul_ws(x, w, l, out_dtype):
    M, K = x.shape
    N = w.shape[-1]
    tm = _tile(M, 1024, V7X_BF16_SUBLANES)
    tn = _tile(N, 512, V7X_LANES)
    return pl.pallas_call(
        _matmul_ws_kernel, grid=(N // tn, M // tm),
        in_specs=[pl.BlockSpec((tm, K), lambda j, i: (i, 0)),
                  pl.BlockSpec((None, K, tn), lambda j, i: (l, 0, j))],
        out_specs=pl.BlockSpec((tm, tn), lambda j, i: (i, j)),
        out_shape=jax.ShapeDtypeStruct((M, N), out_dtype),
        scratch_shapes=[pltpu.VMEM((K, tn), BF16)],
        compiler_params=_cparams(2), name="in_proj")(x, w)


def _conv_post(conv_chunks, lg_ref, lb_ref, o_ref, C):
    s = conv_chunks[0]
    for ch in conv_chunks[1:]:
        s = s + ch
    mu = jnp.sum(s, axis=-1, keepdims=True) / C
    sq = None
    for ch in conv_chunks:
        d = ch - mu
        sq = d * d if sq is None else sq + d * d
    var = jnp.sum(sq, axis=-1, keepdims=True) / C
    inv = lax.rsqrt(var + LN_EPS)
    for c, ch in enumerate(conv_chunks):
        sl = slice(c * V7X_LANES, (c + 1) * V7X_LANES)
        z = (ch - mu) * inv * lg_ref[:, sl] + lb_ref[:, sl]
        o_ref[:, sl] = _silu(z).astype(o_ref.dtype)


def _glu(a_ref, g_ref):
    return a_ref[...].astype(F32) * jax.nn.sigmoid(g_ref[...].astype(F32))


def _conv_prompt_kernel(a_ref, g_ref, ha_ref, hg_ref, w3_ref, cb3_ref, lg_ref, lb_ref,
                        o_ref, nb_ref, upad_ref, conv_ref, *, tt, K, C, rb):
    i = pl.program_id(1)
    nC = C // V7X_LANES
    off = HIST_ROWS - (K - 1)
    u = _glu(a_ref, g_ref)
    uh = jnp.where(i > 0, _glu(ha_ref, hg_ref), 0.0)
    for c in range(nC):
        sl = slice(c * V7X_LANES, (c + 1) * V7X_LANES)
        upad_ref[c, 0:HIST_ROWS, :] = uh[:, sl]
        upad_ref[c, HIST_ROWS:HIST_ROWS + tt, :] = u[:, sl]

    def chunk_body(c, carry):
        for r in range(tt // rb):
            acc = jnp.zeros((rb, V7X_LANES), F32)
            for j in range(K):
                lo = r * rb + off + j
                acc = acc + upad_ref[c, lo:lo + rb, :] * w3_ref[c, j:j + 1, :]
            conv_ref[c, r * rb:(r + 1) * rb, :] = acc + cb3_ref[c]
        return carry

    lax.fori_loop(0, nC, chunk_body, 0)
    _conv_post([conv_ref[c] for c in range(nC)], lg_ref, lb_ref, o_ref, C)

    @pl.when(i == pl.num_programs(1) - 1)
    def _():
        for c in range(nC):
            sl = slice(c * V7X_LANES, (c + 1) * V7X_LANES)
            nb_ref[:, sl] = upad_ref[c, HIST_ROWS + tt - (K - 1):HIST_ROWS + tt, :]


def conv_prompt(proj, w3, cb3, lg, lb, B, T, C, K, l, hc, nb_prev):
    M = proj.shape[0]
    L = nb_prev.shape[0]
    tt = _tile(T, 256, HIST_ROWS)
    nT = T // tt
    nC = C // V7X_LANES
    hb = tt // HIST_ROWS
    kern = functools.partial(_conv_prompt_kernel, tt=tt, K=K, C=C, rb=_tile(tt, 64))
    hist = lambda col: (lambda b, i: (jnp.maximum((b * nT + i) * hb - 1, 0), col))
    in_specs = [pl.BlockSpec((tt, C), lambda b, i: (b * nT + i, 0)),
                pl.BlockSpec((tt, C), lambda b, i: (b * nT + i, 1)),
                pl.BlockSpec((HIST_ROWS, C), hist(0)),
                pl.BlockSpec((HIST_ROWS, C), hist(1)),
                pl.BlockSpec((nC, K, V7X_LANES), lambda b, i: (0, 0, 0)),
                pl.BlockSpec((nC, 1, V7X_LANES), lambda b, i: (0, 0, 0)),
                pl.BlockSpec((1, C), lambda b, i: (0, 0)),
                pl.BlockSpec((1, C), lambda b, i: (0, 0))]
    args = [proj, proj, proj, proj, w3, cb3, lg, lb, hc, nb_prev]
    in_specs += [ANY, ANY]
    aliases = {8: 0, 9: 1}
    return pl.pallas_call(
        _skip_refs(kern, 8, 2), grid=(B, nT), in_specs=in_specs,
        out_specs=[pl.BlockSpec((tt, C), lambda b, i: (b * nT + i, 0)),
                   pl.BlockSpec((None, None, K - 1, C), lambda b, i: (l, b, 0, 0))],
        out_shape=[jax.ShapeDtypeStruct((M, C), BF16),
                   jax.ShapeDtypeStruct((L, B, K - 1, C), F32)],
        scratch_shapes=[pltpu.VMEM((nC, HIST_ROWS + tt, V7X_LANES), F32),
                        pltpu.VMEM((nC, tt, V7X_LANES), F32)],
        input_output_aliases=aliases,
        compiler_params=_cparams(2), name="conv_prompt")(*args)


def _conv_sample_kernel(a_ref, g_ref, cache_ref, w_ref, cb_ref, lg_ref, lb_ref,
                        o_ref, nb_ref, upad_ref, *, T, K, C, bb):
    nC = C // V7X_LANES
    u = _glu(a_ref, g_ref)
    upad_ref[:, 0:K - 1, :] = cache_ref[...]
    upad_ref[:, K - 1:K - 1 + T, :] = u.reshape(bb, T, C)
    nb_ref[...] = upad_ref[:, T:T + K - 1, :]
    chunks = []
    for c in range(nC):
        sl = slice(c * V7X_LANES, (c + 1) * V7X_LANES)
        acc = jnp.zeros((bb, T, V7X_LANES), F32)
        for j in range(K):
            acc = acc + upad_ref[:, j:j + T, sl] * w_ref[j:j + 1, sl]
        chunks.append(acc.reshape(bb * T, V7X_LANES) + cb_ref[:, sl])
    _conv_post(chunks, lg_ref, lb_ref, o_ref, C)


def conv_sample(proj, cache, l, w, cb, lg, lb, row0, B, T, C, K, hc, nb_prev):
    assert T % 8 == 0
    L = cache.shape[0]
    bb = _tile(B, 16, 2)
    assert row0 % (bb * T) == 0
    blk0 = row0 // (bb * T)
    kern = functools.partial(_conv_sample_kernel, T=T, K=K, C=C, bb=bb)
    vec = pl.BlockSpec((1, C), lambda s: (0, 0))
    in_specs = [pl.BlockSpec((bb * T, C), lambda s: (blk0 + s, 0)),
                pl.BlockSpec((bb * T, C), lambda s: (blk0 + s, 1)),
                pl.BlockSpec((None, bb, K - 1, C), lambda s: (l, s, 0, 0)),
                pl.BlockSpec((K, C), lambda s: (0, 0)), vec, vec, vec, ANY]
    in_specs.append(ANY)
    args = [proj, proj, cache, w, cb, lg, lb, hc, nb_prev]
    aliases = {7: 0, 8: 1}
    return pl.pallas_call(
        _skip_refs(kern, 7, 2), grid=(B // bb,), in_specs=in_specs,
        out_specs=[pl.BlockSpec((bb * T, C), lambda s: (blk0 + s, 0)),
                   pl.BlockSpec((None, bb, K - 1, C), lambda s: (l, s, 0, 0))],
        out_shape=[jax.ShapeDtypeStruct(hc.shape, BF16),
                   jax.ShapeDtypeStruct((L, B, K - 1, C), F32)],
        scratch_shapes=[pltpu.VMEM((bb, K - 1 + T, C), F32)],
        input_output_aliases=aliases,
        compiler_params=_cparams(1), name="conv_sample")(*args)


def _retention_tables(H, chunk, rows):
    gamma = (1.0 - 2.0 ** (-5.0 - np.arange(H, dtype=np.float32))).astype(np.float32)
    log_g = np.log(gamma).astype(np.float32)
    idx = np.arange(chunk, dtype=np.float32)
    diff = idx[:, None] - idx[None, :]
    dmask = np.where(diff[None] >= 0, np.exp(diff[None] * log_g[:, None, None]), 0.0).astype(np.float32)
    q_dec = np.exp((idx[None, :] + np.float32(1.0)) * log_g[:, None]).astype(np.float32)
    k_dec = np.exp((np.float32(chunk) - np.float32(1.0) - idx[None, :]) * log_g[:, None]).astype(np.float32)
    c_dec = np.exp(np.float32(chunk) * log_g).astype(np.float32)
    dm = np.zeros((H, rows, rows), np.float32)
    dm[:, :chunk, :chunk] = dmask
    qd = np.zeros((H, rows, 1), np.float32)
    qd[:, :chunk, 0] = q_dec
    kd = np.zeros((H, rows, 1), np.float32)
    kd[:, :chunk, 0] = k_dec
    return dm, qd, kd, [float(c) for c in c_dec]


def _rotary_tables(pos, half):
    inv = (1.0 / (ROPE_BASE ** (np.arange(half, dtype=np.float32) / half))).astype(np.float32)
    ang = (pos.astype(np.float32)[:, None] * inv[None, :]).astype(np.float32)
    return np.cos(ang).astype(np.float32), np.sin(ang).astype(np.float32)


def _retention_kernel(*refs, H, DK, C, nb, rows, c_dec, has_s0):
    if has_s0:
        (q_ref, k_ref, v_ref, g_ref, cos_ref, sin_ref, dm_ref, qd_ref, kd_ref, gng_ref, gnb_ref,
         s0_ref, o_ref, ns_ref) = refs
    else:
        (q_ref, k_ref, v_ref, g_ref, cos_ref, sin_ref, dm_ref, qd_ref, kd_ref, gng_ref, gnb_ref,
         o_ref, ns_ref) = refs
        s0_ref = None
    half = DK // 2
    scale = DK ** -0.5

    @pl.when(pl.program_id(1) == 0)
    def _():
        if has_s0:
            ns_ref[...] = s0_ref[...]
        else:
            ns_ref[...] = jnp.zeros(ns_ref.shape, F32)

    cos = cos_ref[...]
    sin = sin_ref[...]
    q_all = q_ref[...].astype(F32)
    k_all = k_ref[...].astype(F32)
    v_all = v_ref[...].astype(F32)
    g_all = g_ref[...].astype(F32)

    def rot(x):
        x1, x2 = x[:, :half], x[:, half:]
        return jnp.concatenate([x1 * cos - x2 * sin, x1 * sin + x2 * cos], axis=-1)

    def pad_rows(x):
        if rows == C:
            return x
        return jnp.concatenate([x, jnp.zeros((rows - C, x.shape[1]), x.dtype)], axis=0)

    outs = []
    for s in range(nb):
        rs = slice(s * C, (s + 1) * C)
        heads = []
        for h in range(H):
            hs = slice(h * DK, (h + 1) * DK)
            qr = pad_rows(rot(q_all[rs, hs]))
            kr = pad_rows(rot(k_all[rs, hs]) * scale)
            vb = pad_rows(v_all[rs, hs]).astype(BF16)
            S = ns_ref[s, h]
            scores = lax.dot_general(qr.astype(BF16), kr.astype(BF16), (((1,), (1,)), ((), ())),
                                     preferred_element_type=F32) * dm_ref[h]
            o = (jnp.dot(scores.astype(BF16), vb, preferred_element_type=F32)
                 + jnp.dot((qr * qd_ref[h]).astype(BF16), S.astype(BF16), preferred_element_type=F32))
            kd = (kr * kd_ref[h]).astype(BF16)
            ns_ref[s, h] = c_dec[h] * S + lax.dot_general(kd, vb, (((0,), (0,)), ((), ())),
                                                          preferred_element_type=F32)
            o = o[:C]
            mu = jnp.mean(o, axis=-1, keepdims=True)
            d = o - mu
            var = jnp.mean(d * d, axis=-1, keepdims=True)
            on = d * lax.rsqrt(var + LN_EPS) * gng_ref[:, hs] + gnb_ref[:, hs]
            heads.append(_silu(g_all[rs, hs]) * on)
        outs.append(jnp.concatenate(heads, axis=-1))
    o_ref[...] = jnp.concatenate(outs, axis=0).astype(o_ref.dtype)


def retention(proj, gng, gnb, row0, B, T, H, DK, pos0, l, hr, ns_prev, state=None):
    M = proj.shape[0]
    L = ns_prev.shape[0]
    R = H * DK
    C = int(np.gcd(T, RET_CHUNK))
    n_chunks = T // C
    nb = 1 if n_chunks > 1 else _tile(B, 4, 1)
    rows = max(C, V7X_BF16_SUBLANES)
    dm, qd, kd, c_dec = _retention_tables(H, C, rows)
    cos, sin = _rotary_tables(pos0 + np.arange(T), DK // 2)
    blk_rows = nb * C
    assert row0 % blk_rows == 0 and (blk_rows % V7X_BF16_SUBLANES == 0)
    blk0 = row0 // blk_rows
    has_s0 = state is not None
    kern = functools.partial(_retention_kernel, H=H, DK=DK, C=C, nb=nb, rows=rows,
                             c_dec=c_dec, has_s0=has_s0)

    def row_map(col):
        return lambda sb, c: (blk0 + sb * n_chunks + c, col)

    in_specs = [pl.BlockSpec((blk_rows, R), row_map(2)),
                pl.BlockSpec((blk_rows, R), row_map(3)),
                pl.BlockSpec((blk_rows, R), row_map(4)),
                pl.BlockSpec((blk_rows, R), row_map(5)),
                pl.BlockSpec((C, DK // 2), lambda sb, c: (c, 0)),
                pl.BlockSpec((C, DK // 2), lambda sb, c: (c, 0)),
                pl.BlockSpec((H, rows, rows), lambda sb, c: (0, 0, 0)),
                pl.BlockSpec((H, rows, 1), lambda sb, c: (0, 0, 0)),
                pl.BlockSpec((H, rows, 1), lambda sb, c: (0, 0, 0)),
                pl.BlockSpec((1, R), lambda sb, c: (0, 0)),
                pl.BlockSpec((1, R), lambda sb, c: (0, 0))]
    args = [proj, proj, proj, proj, jnp.asarray(cos), jnp.asarray(sin), jnp.asarray(dm),
            jnp.asarray(qd), jnp.asarray(kd), gng, gnb]
    if has_s0:
        in_specs.append(pl.BlockSpec((None, nb, H, DK, DK), lambda sb, c: (l, sb, 0, 0, 0)))
        args.append(state)
    n_used = len(args)
    aliases = {n_used: 0, n_used + 1: 1}
    in_specs += [ANY, ANY]
    args += [hr, ns_prev]
    return pl.pallas_call(
        _skip_refs(kern, n_used, 2), grid=(B // nb, n_chunks),
        in_specs=in_specs,
        out_specs=[pl.BlockSpec((blk_rows, R), lambda sb, c: (blk0 + sb * n_chunks + c, 0)),
                   pl.BlockSpec((None, nb, H, DK, DK), lambda sb, c: (l, sb, 0, 0, 0))],
        out_shape=[jax.ShapeDtypeStruct((M, R), BF16),
                   jax.ShapeDtypeStruct((L, B, H, DK, DK), F32)],
        input_output_aliases=aliases,
        compiler_params=_cparams(2), name="retention")(*args)


def _outproj_ln_kernel(hc_ref, hr_ref, x_ref, w_ref, g_ref, b_ref, xo_ref, xb_ref, *, alpha, cc):
    m = (jnp.dot(hc_ref[...], w_ref[0:cc, :], preferred_element_type=F32)
         + jnp.dot(hr_ref[...], w_ref[cc:, :], preferred_element_type=F32))
    z = _layer_norm(alpha * x_ref[...] + m, g_ref[...], b_ref[...])
    xo_ref[...] = z
    xb_ref[...] = z.astype(BF16)


def outproj_ln(hc, hr, x, wb, g, b, alpha):
    M, D = x.shape
    cc = hc.shape[1]
    tm = _tile(M, 512, V7X_BF16_SUBLANES)
    kern = functools.partial(_outproj_ln_kernel, alpha=alpha, cc=cc)
    return pl.pallas_call(
        kern, grid=(M // tm,),
        in_specs=[pl.BlockSpec((tm, cc), lambda i: (i, 0)),
                  pl.BlockSpec((tm, hr.shape[1]), lambda i: (i, 0)),
                  pl.BlockSpec((tm, D), lambda i: (i, 0)),
                  pl.BlockSpec(wb.shape, lambda i: (0, 0)),
                  pl.BlockSpec((1, D), lambda i: (0, 0)),
                  pl.BlockSpec((1, D), lambda i: (0, 0))],
        out_specs=[pl.BlockSpec((tm, D), lambda i: (i, 0)),
                   pl.BlockSpec((tm, D), lambda i: (i, 0))],
        out_shape=[jax.ShapeDtypeStruct((M, D), F32), jax.ShapeDtypeStruct((M, D), BF16)],
        compiler_params=_cparams(1), name="outproj_ln")(hc, hr, x, wb, g, b)


def _resid_ln_kernel(f_ref, x_ref, g_ref, b_ref, xo_ref, xb_ref, *, alpha):
    z = _layer_norm(alpha * x_ref[...] + f_ref[...], g_ref[...], b_ref[...])
    xo_ref[...] = z
    xb_ref[...] = z.astype(BF16)


def resid_ln(f, x, g, b, alpha):
    M, D = x.shape
    tm = _tile(M, 512, V7X_BF16_SUBLANES)
    kern = functools.partial(_resid_ln_kernel, alpha=alpha)
    row = pl.BlockSpec((tm, D), lambda i: (i, 0))
    vec = pl.BlockSpec((1, D), lambda i: (0, 0))
    return pl.pallas_call(
        kern, grid=(M // tm,), in_specs=[row, row, vec, vec], out_specs=[row, row],
        out_shape=[jax.ShapeDtypeStruct((M, D), F32), jax.ShapeDtypeStruct((M, D), BF16)],
        compiler_params=_cparams(1), name="resid_ln")(f, x, g, b)


def _tile_state(te_ref, na_ref):
    i = pl.program_id(1)
    active = i < na_ref[0]
    changed = jnp.logical_or(i == 0, te_ref[i] != te_ref[jnp.maximum(i - 1, 0)])
    return active, changed


def _weight_stream(te_ref, na_ref, tend_ref, copies, cast):
    j = pl.program_id(0)
    i = pl.program_id(1)
    active, changed = _tile_state(te_ref, na_ref)

    @pl.when(jnp.logical_and(active, changed))
    def _():
        e = te_ref[i]

        @pl.when(jnp.logical_and(j == 0, i == 0))
        def _():
            for c in copies(e, j):
                c.start()
        for c in copies(e, j):
            c.wait()
        cast()
        nxt = tend_ref[e]
        same_j = nxt < na_ref[0]
        e2 = te_ref[jnp.where(same_j, nxt, 0)]
        j2 = jnp.where(same_j, j, j + 1)

        @pl.when(j2 < pl.num_programs(0))
        def _():
            for c in copies(e2, j2):
                c.start()
    return active


def _gateup_kernel(te_ref, na_ref, tend_ref, x_ref, wg_hbm, wu_hbm, o_ref,
                   sg_ref, su_ref, wgb_ref, wub_ref, sem, *, li, tn):
    def copies(e, j):
        cols = pl.ds(pl.multiple_of(j * tn, tn), tn)
        return (pltpu.make_async_copy(wg_hbm.at[li, e, :, cols], sg_ref, sem.at[0]),
                pltpu.make_async_copy(wu_hbm.at[li, e, :, cols], su_ref, sem.at[1]))

    def cast():
        wgb_ref[...] = sg_ref[...].astype(BF16)
        wub_ref[...] = su_ref[...].astype(BF16)

    active = _weight_stream(te_ref, na_ref, tend_ref, copies, cast)

    @pl.when(active)
    def _():
        x = x_ref[...]
        gate = jnp.dot(x, wgb_ref[...], preferred_element_type=F32)
        up = jnp.dot(x, wub_ref[...], preferred_element_type=F32)
        o_ref[...] = (_silu(gate) * up).astype(o_ref.dtype)

    @pl.when(jnp.logical_not(active))
    def _():
        o_ref[...] = jnp.zeros(o_ref.shape, o_ref.dtype)


def _down_kernel(te_ref, na_ref, tend_ref, h_ref, w_hbm, o_ref, s_ref, wb_ref, sem, *, li, tn):
    def copies(e, j):
        cols = pl.ds(pl.multiple_of(j * tn, tn), tn)
        return (pltpu.make_async_copy(w_hbm.at[li, e, :, cols], s_ref, sem.at[0]),)

    def cast():
        wb_ref[...] = s_ref[...].astype(BF16)

    active = _weight_stream(te_ref, na_ref, tend_ref, copies, cast)

    @pl.when(active)
    def _():
        o_ref[...] = jnp.dot(h_ref[...], wb_ref[...], preferred_element_type=F32)

    @pl.when(jnp.logical_not(active))
    def _():
        o_ref[...] = jnp.zeros(o_ref.shape, o_ref.dtype)


def grouped_swiglu(xs, wg, wu, wd, li, tile_expert, n_active, tile_end, tm):
    S, D = xs.shape
    Fd = wg.shape[-1]
    nT = S // tm
    tn = _tile(Fd, 512, V7X_LANES)

    def row_map(j, i, te, na, tend):
        return (jnp.minimum(i, na[0] - 1), 0)

    def out_map(j, i, te, na, tend):
        return (i, j)

    h = pl.pallas_call(
        functools.partial(_gateup_kernel, li=li, tn=tn),
        grid_spec=pltpu.PrefetchScalarGridSpec(
            num_scalar_prefetch=3, grid=(Fd // tn, nT),
            in_specs=[pl.BlockSpec((tm, D), row_map), ANY, ANY],
            out_specs=pl.BlockSpec((tm, tn), out_map),
            scratch_shapes=[pltpu.VMEM((D, tn), F32), pltpu.VMEM((D, tn), F32),
                            pltpu.VMEM((D, tn), BF16), pltpu.VMEM((D, tn), BF16),
                            pltpu.SemaphoreType.DMA((2,))]),
        out_shape=jax.ShapeDtypeStruct((S, Fd), BF16),
        compiler_params=_cparams(2), name="ffn_gate_up")(tile_expert, n_active, tile_end, xs, wg, wu)

    tn2 = _tile(D, 512, V7X_LANES)
    return pl.pallas_call(
        functools.partial(_down_kernel, li=li, tn=tn2),
        grid_spec=pltpu.PrefetchScalarGridSpec(
            num_scalar_prefetch=3, grid=(D // tn2, nT),
            in_specs=[pl.BlockSpec((tm, Fd), row_map), ANY],
            out_specs=pl.BlockSpec((tm, tn2), out_map),
            scratch_shapes=[pltpu.VMEM((Fd, tn2), F32), pltpu.VMEM((Fd, tn2), BF16),
                            pltpu.SemaphoreType.DMA((1,))]),
        out_shape=jax.ShapeDtypeStruct((S, D), F32),
        compiler_params=_cparams(2), name="ffn_down")(tile_expert, n_active, tile_end, h, wd)


def _router_kernel(x_ref, w_ref, b_ref, idx_ref, gate_ref):
    logits = jnp.dot(x_ref[...], w_ref[...], precision=lax.Precision.HIGHEST,
                     preferred_element_type=F32) + b_ref[...]
    col = lax.broadcasted_iota(jnp.int32, logits.shape, 1)
    big = jnp.int32(logits.shape[1])
    m1 = jnp.max(logits, axis=-1, keepdims=True)
    i1 = jnp.min(jnp.where(logits == m1, col, big), axis=-1, keepdims=True)
    rest = jnp.where(col == i1, -jnp.inf, logits)
    m2 = jnp.max(rest, axis=-1, keepdims=True)
    i2 = jnp.min(jnp.where(rest == m2, col, big), axis=-1, keepdims=True)
    e = jnp.exp(m2 - m1)
    g1 = 1.0 / (1.0 + e)
    g2 = e / (1.0 + e)
    idx_ref[...] = jnp.where(col == 0, i1, jnp.where(col == 1, i2, 0))
    gate_ref[...] = jnp.where(col == 0, g1, jnp.where(col == 1, g2, 0.0))


def router(x, w_pad, b_pad):
    M, D = x.shape
    tm = _tile(M, 512)
    out = pl.BlockSpec((tm, V7X_LANES), lambda i: (i, 0))
    return pl.pallas_call(
        _router_kernel, grid=(M // tm,),
        in_specs=[pl.BlockSpec((tm, D), lambda i: (i, 0)),
                  pl.BlockSpec((D, V7X_LANES), lambda i: (0, 0)),
                  pl.BlockSpec((1, V7X_LANES), lambda i: (0, 0))],
        out_specs=[out, out],
        out_shape=[jax.ShapeDtypeStruct((M, V7X_LANES), jnp.int32),
                   jax.ShapeDtypeStruct((M, V7X_LANES), F32)],
        compiler_params=_cparams(1), name="router")(x, w_pad, b_pad)


ISSUE_UNROLL = 8


def _dispatch_kernel(src_ref, na_ref, x_hbm, o_ref, buf_ref, sem, *, R):
    i = pl.program_id(0)
    na = na_ref[0]

    def issue_tile(t, slot):
        def body(r, carry):
            tok = src_ref[t * R + r]
            pltpu.make_async_copy(x_hbm.at[pl.ds(tok, 1), :], buf_ref.at[slot, pl.ds(r, 1), :],
                                  sem.at[slot]).start()
            return carry
        lax.fori_loop(0, R, body, 0, unroll=ISSUE_UNROLL)

    @pl.when(i == 0)
    def _():
        issue_tile(0, 0)

    @pl.when(i + 1 < na)
    def _():
        issue_tile(i + 1, (i + 1) % 2)

    @pl.when(i < na)
    def _():
        slot = i % 2
        pltpu.make_async_copy(x_hbm.at[pl.ds(0, R), :], buf_ref.at[slot], sem.at[slot]).wait()
        o_ref[...] = buf_ref[slot].astype(o_ref.dtype)

    @pl.when(i >= na)
    def _():
        o_ref[...] = jnp.zeros(o_ref.shape, o_ref.dtype)


def dispatch(x, src, n_active, S, R):
    M, D = x.shape
    kern = functools.partial(_dispatch_kernel, R=R)
    return pl.pallas_call(
        kern,
        grid_spec=pltpu.PrefetchScalarGridSpec(
            num_scalar_prefetch=2, grid=(S // R,),
            in_specs=[ANY],
            out_specs=pl.BlockSpec((R, D), lambda i, src, na: (i, 0)),
            scratch_shapes=[pltpu.VMEM((2, R, D), F32), pltpu.SemaphoreType.DMA((2,))]),
        out_shape=jax.ShapeDtypeStruct((S, D), BF16),
        compiler_params=_cparams(1), name="moe_dispatch")(src, n_active, x)


def _combine_ln_kernel(pos_ref, ys_hbm, x_ref, gate_ref, g_ref, b_ref, xo_ref, xb_ref, buf_ref, sem,
                       *, R, alpha):
    i = pl.program_id(0)

    def issue_tile(t, slot):
        def body(r, carry):
            a = (t * R + r) * TOP_K
            for k in range(TOP_K):
                pltpu.make_async_copy(ys_hbm.at[pl.ds(pos_ref[a + k], 1), :],
                                      buf_ref.at[slot, k, pl.ds(r, 1), :], sem.at[slot]).start()
            return carry
        lax.fori_loop(0, R, body, 0, unroll=ISSUE_UNROLL // TOP_K)

    @pl.when(i == 0)
    def _():
        issue_tile(0, 0)

    @pl.when(i + 1 < pl.num_programs(0))
    def _():
        issue_tile(i + 1, (i + 1) % 2)

    slot = i % 2
    for k in range(TOP_K):
        pltpu.make_async_copy(ys_hbm.at[pl.ds(0, R), :], buf_ref.at[slot, k], sem.at[slot]).wait()
    gates = gate_ref[...]
    y = gates[:, 0:1] * buf_ref[slot, 0]
    for k in range(1, TOP_K):
        y = y + gates[:, k:k + 1] * buf_ref[slot, k]
    z = _layer_norm(alpha * x_ref[...] + y, g_ref[...], b_ref[...])
    xo_ref[...] = z
    xb_ref[...] = z.astype(BF16)


def combine_ln(ys, pos, x, gates, g, b, alpha):
    M, D = x.shape
    R = _tile(M, 256, V7X_BF16_SUBLANES)
    kern = functools.partial(_combine_ln_kernel, R=R, alpha=alpha)
    row = lambda i, pos: (i, 0)
    fix = lambda i, pos: (0, 0)
    return pl.pallas_call(
        kern,
        grid_spec=pltpu.PrefetchScalarGridSpec(
            num_scalar_prefetch=1, grid=(M // R,),
            in_specs=[ANY,
                      pl.BlockSpec((R, D), row),
                      pl.BlockSpec((R, V7X_LANES), row),
                      pl.BlockSpec((1, D), fix),
                      pl.BlockSpec((1, D), fix)],
            out_specs=[pl.BlockSpec((R, D), row), pl.BlockSpec((R, D), row)],
            scratch_shapes=[pltpu.VMEM((2, TOP_K, R, D), F32), pltpu.SemaphoreType.DMA((2,))]),
        out_shape=[jax.ShapeDtypeStruct((M, D), F32), jax.ShapeDtypeStruct((M, D), BF16)],
        compiler_params=_cparams(1), name="moe_combine_ln")(pos, ys, x, gates, g, b)


def _routing_plan(idx, E, tm, n_tiles):
    e_flat = idx.reshape(-1)
    A = e_flat.shape[0]
    onehot = (e_flat[:, None] == jnp.arange(E, dtype=jnp.int32)[None, :]).astype(jnp.int32)
    csum = jnp.cumsum(onehot, axis=0)
    counts = csum[-1]
    rank = jnp.take_along_axis(csum, e_flat[:, None], axis=1)[:, 0] - 1
    tiles_e = (counts + tm - 1) // tm
    tile_end = jnp.cumsum(tiles_e).astype(jnp.int32)
    pad_start = (tile_end - tiles_e) * tm
    start = jnp.cumsum(counts) - counts
    n_active = tile_end[-1]
    pos = pad_start[e_flat] + rank
    tile_id = jnp.arange(n_tiles, dtype=jnp.int32)
    te = jnp.searchsorted(tile_end, jnp.minimum(tile_id, n_active - 1), side="right").astype(jnp.int32)
    order = jnp.argsort(e_flat, stable=True).astype(jnp.int32)
    slot = jnp.arange(n_tiles * tm, dtype=jnp.int32)
    se = te[slot // tm]
    off = slot - pad_start[se]
    r = start[se] + jnp.clip(off, 0, jnp.maximum(counts[se] - 1, 0))
    src = order[jnp.clip(r, 0, A - 1)] // TOP_K
    return (src.astype(jnp.int32), pos.astype(jnp.int32), te,
            n_active.reshape(1).astype(jnp.int32), tile_end)


def kernel(x_prompt, x_sample, cache_conv, state_ret, w_in, conv_w, conv_b, conv_ln_g, conv_ln_b,
           ret_gn_g, ret_gn_b, w_out, ln1_g, ln1_b, ln2_g, ln2_b, w_ff_gate, w_ff_up, w_ff_down,
           w_router, b_router, w_exp_gate, w_exp_up, w_exp_down):
    B, T, D = x_prompt.shape
    Bs, Ts, _ = x_sample.shape
    L = w_in.shape[0]
    K = conv_w.shape[1]
    C = conv_w.shape[2]
    H, DK = state_ret.shape[2], state_ret.shape[3]
    E = w_router.shape[-1]
    Np, Ns = B * T, Bs * Ts
    M = Np + Ns
    nC = C // V7X_LANES
    alpha = float((2 * L) ** 0.25)
    assert w_in.shape[-1] == 6 * C and H * DK == C and D == 2 * C

    x = jnp.concatenate([x_prompt.reshape(Np, D), x_sample.reshape(Ns, D)], axis=0)
    xb = x.astype(BF16)

    tm_dense = _tile(M, 1024, V7X_BF16_SUBLANES)
    n_dense_tiles = M // tm_dense
    dense_te = jnp.zeros((n_dense_tiles,), jnp.int32)
    dense_na = jnp.full((1,), n_dense_tiles, jnp.int32)
    w_ff_gate4, w_ff_up4, w_ff_down4 = w_ff_gate[:, None], w_ff_up[:, None], w_ff_down[:, None]
    moe_tm = _tile(M * TOP_K, MOE_TILE, V7X_BF16_SUBLANES)
    moe_tiles = (M * TOP_K) // moe_tm + E

    hc = jnp.zeros((M, C), BF16)
    hr = jnp.zeros((M, H * DK), BF16)
    conv_p = jnp.zeros((L, B, K - 1, C), F32)
    conv_s = jnp.zeros((L, Bs, K - 1, C), F32)
    ret_p = jnp.zeros((L, B, H, DK, DK), F32)
    ret_s = jnp.zeros((L, Bs, H, DK, DK), F32)
    for l in range(L):
        proj = matmul_ws(xb, w_in, l, BF16)

        w3 = conv_w[l].reshape(K, nC, V7X_LANES).transpose(1, 0, 2)
        cb3 = conv_b[l].reshape(nC, 1, V7X_LANES)
        cb, lg, lb = conv_b[l][None], conv_ln_g[l][None], conv_ln_b[l][None]
        hc, conv_p = conv_prompt(proj, w3, cb3, lg, lb, B, T, C, K, l, hc, conv_p)
        hc, conv_s = conv_sample(proj, cache_conv, l, conv_w[l], cb, lg, lb, Np, Bs, Ts, C, K, hc, conv_s)

        gng, gnb = ret_gn_g[l][None], ret_gn_b[l][None]
        hr, ret_p = retention(proj, gng, gnb, 0, B, T, H, DK, 0, l, hr, ret_p)
        hr, ret_s = retention(proj, gng, gnb, Np, Bs, Ts, H, DK, PAST_LEN, l, hr, ret_s, state=state_ret)

        x, xb = outproj_ln(hc, hr, x, cast_weight(w_out, l), ln1_g[l][None], ln1_b[l][None], alpha)

        if l % 2 == 0:
            f = grouped_swiglu(xb, w_ff_gate4, w_ff_up4, w_ff_down4, l // 2, dense_te, dense_na,
                               dense_na, tm_dense)
            x, xb = resid_ln(f, x, ln2_g[l][None], ln2_b[l][None], alpha)
        else:
            li = l // 2
            w_pad = jnp.zeros((D, V7X_LANES), F32).at[:, :E].set(w_router[li])
            b_pad = jnp.full((1, V7X_LANES), -1e30, F32).at[0, :E].set(b_router[li])
            idx_w, gate_w = router(x, w_pad, b_pad)
            src, pos, te, na, tend = _routing_plan(idx_w[:, :TOP_K], E, moe_tm, moe_tiles)
            xs = dispatch(x, src, na, moe_tiles * moe_tm, moe_tm)
            ys = grouped_swiglu(xs, w_exp_gate, w_exp_up, w_exp_down, li, te, na, tend, moe_tm)
            x, xb = combine_ln(ys, pos, x, gate_w, ln2_g[l][None], ln2_b[l][None], alpha)

    y_prompt = x[:Np].reshape(B, T, D)
    y_sample = x[Np:].reshape(Bs, Ts, D)
    return (y_prompt, y_sample, conv_p, ret_p, conv_s, ret_s)
```

```python
import functools

import numpy as np
import jax
import jax.numpy as jnp
from jax import lax
from jax.experimental import pallas as pl
from jax.experimental.pallas import tpu as pltpu

F32 = jnp.float32
BF16 = jnp.bfloat16

LN_EPS = 1e-5
ROPE_BASE = 10000.0
RET_CHUNK = 128
PAST_LEN = 16384
TOP_K = 2

V7X_LANES = 128
V7X_BF16_SUBLANES = 16
V7X_VMEM_LIMIT_BYTES = 56 * 1024 * 1024

HIST_ROWS = 32
MOE_TILE = 512
RET_SEQS_PER_STEP = 4
RET_CHUNKS_PER_STEP = 4
ANY = pl.BlockSpec(memory_space=pl.ANY)


def _cparams(n_axes):
    return pltpu.CompilerParams(dimension_semantics=("arbitrary",) * n_axes,
                                vmem_limit_bytes=V7X_VMEM_LIMIT_BYTES)


def _tile(n, pref, mult=8):
    t = min(pref, n)
    while t > mult and (n % t or t % mult):
        t -= mult
    assert n % t == 0 and t % mult == 0, (n, pref, mult)
    return t


def _skip_refs(kern, start, count):
    def wrapped(*refs):
        return kern(*refs[:start], *refs[start + count:])
    return wrapped


def _layer_norm(y, g, b):
    mu = jnp.mean(y, axis=-1, keepdims=True)
    d = y - mu
    var = jnp.mean(d * d, axis=-1, keepdims=True)
    return d * lax.rsqrt(var + LN_EPS) * g + b


def _silu(z):
    return z * jax.nn.sigmoid(z)


def _cast_kernel(w_ref, o_ref):
    o_ref[...] = w_ref[...].astype(o_ref.dtype)


def cast_weight(w, l):
    _, K, N = w.shape
    tr = _tile(K, 512, V7X_BF16_SUBLANES)
    return pl.pallas_call(
        _cast_kernel, grid=(K // tr,),
        in_specs=[pl.BlockSpec((None, tr, N), lambda i: (l, i, 0))],
        out_specs=pl.BlockSpec((tr, N), lambda i: (i, 0)),
        out_shape=jax.ShapeDtypeStruct((K, N), BF16),
        compiler_params=_cparams(1), name="cast_weight")(w)


def _matmul_ws_kernel(x_ref, w_ref, o_ref, wb_ref):
    @pl.when(pl.program_id(1) == 0)
    def _():
        wb_ref[...] = w_ref[...].astype(BF16)
    o_ref[...] = jnp.dot(x_ref[...], wb_ref[...], preferred_element_type=F32).astype(o_ref.dtype)


def matmul_ws(x, w, l, out_dtype):
    M, K = x.shape
    N = w.shape[-1]
    tm = _tile(M, 1024, V7X_BF16_SUBLANES)
    tn = _tile(N, 1024, V7X_LANES)
    return pl.pallas_call(
        _matmul_ws_kernel, grid=(N // tn, M // tm),
        in_specs=[pl.BlockSpec((tm, K), lambda j, i: (i, 0)),
                  pl.BlockSpec((None, K, tn), lambda j, i: (l, 0, j))],
        out_specs=pl.BlockSpec((tm, tn), lambda j, i: (i, j)),
        out_shape=jax.ShapeDtypeStruct((M, N), out_dtype),
        scratch_shapes=[pltpu.VMEM((K, tn), BF16)],
        compiler_params=_cparams(2), name="in_proj")(x, w)


def _conv_post(conv_chunks, lg_ref, lb_ref, o_ref, C):
    s = conv_chunks[0]
    for ch in conv_chunks[1:]:
        s = s + ch
    mu = jnp.sum(s, axis=-1, keepdims=True) / C
    sq = None
    for ch in conv_chunks:
        d = ch - mu
        sq = d * d if sq is None else sq + d * d
    var = jnp.sum(sq, axis=-1, keepdims=True) / C
    inv = lax.rsqrt(var + LN_EPS)
    for c, ch in enumerate(conv_chunks):
        sl = slice(c * V7X_LANES, (c + 1) * V7X_LANES)
        z = (ch - mu) * inv * lg_ref[:, sl] + lb_ref[:, sl]
        o_ref[:, sl] = _silu(z).astype(o_ref.dtype)


def _glu(a_ref, g_ref):
    return a_ref[...].astype(F32) * jax.nn.sigmoid(g_ref[...].astype(F32))


def _conv_prompt_kernel(a_ref, g_ref, ha_ref, hg_ref, w3_ref, cb3_ref, lg_ref, lb_ref,
                        o_ref, nb_ref, upad_ref, conv_ref, *, tt, K, C, rb):
    i = pl.program_id(1)
    nC = C // V7X_LANES
    off = HIST_ROWS - (K - 1)
    u = _glu(a_ref, g_ref)
    uh = jnp.where(i > 0, _glu(ha_ref, hg_ref), 0.0)
    for c in range(nC):
        sl = slice(c * V7X_LANES, (c + 1) * V7X_LANES)
        upad_ref[c, 0:HIST_ROWS, :] = uh[:, sl]
        upad_ref[c, HIST_ROWS:HIST_ROWS + tt, :] = u[:, sl]

    def chunk_body(c, carry):
        for r in range(tt // rb):
            acc = jnp.zeros((rb, V7X_LANES), F32)
            for j in range(K):
                lo = r * rb + off + j
                acc = acc + upad_ref[c, lo:lo + rb, :] * w3_ref[c, j:j + 1, :]
            conv_ref[c, r * rb:(r + 1) * rb, :] = acc + cb3_ref[c]
        return carry

    lax.fori_loop(0, nC, chunk_body, 0)
    _conv_post([conv_ref[c] for c in range(nC)], lg_ref, lb_ref, o_ref, C)

    @pl.when(i == pl.num_programs(1) - 1)
    def _():
        for c in range(nC):
            sl = slice(c * V7X_LANES, (c + 1) * V7X_LANES)
            nb_ref[:, sl] = upad_ref[c, HIST_ROWS + tt - (K - 1):HIST_ROWS + tt, :]


def conv_prompt(proj, w3, cb3, lg, lb, B, T, C, K, l, hc, nb_prev):
    M = proj.shape[0]
    L = nb_prev.shape[0]
    tt = _tile(T, 256, HIST_ROWS)
    nT = T // tt
    nC = C // V7X_LANES
    hb = tt // HIST_ROWS
    kern = functools.partial(_conv_prompt_kernel, tt=tt, K=K, C=C, rb=_tile(tt, 64))
    hist = lambda col: (lambda b, i: (jnp.maximum((b * nT + i) * hb - 1, 0), col))
    in_specs = [pl.BlockSpec((tt, C), lambda b, i: (b * nT + i, 0)),
                pl.BlockSpec((tt, C), lambda b, i: (b * nT + i, 1)),
                pl.BlockSpec((HIST_ROWS, C), hist(0)),
                pl.BlockSpec((HIST_ROWS, C), hist(1)),
                pl.BlockSpec((nC, K, V7X_LANES), lambda b, i: (0, 0, 0)),
                pl.BlockSpec((nC, 1, V7X_LANES), lambda b, i: (0, 0, 0)),
                pl.BlockSpec((1, C), lambda b, i: (0, 0)),
                pl.BlockSpec((1, C), lambda b, i: (0, 0))]
    args = [proj, proj, proj, proj, w3, cb3, lg, lb, hc, nb_prev]
    in_specs += [ANY, ANY]
    aliases = {8: 0, 9: 1}
    return pl.pallas_call(
        _skip_refs(kern, 8, 2), grid=(B, nT), in_specs=in_specs,
        out_specs=[pl.BlockSpec((tt, C), lambda b, i: (b * nT + i, 0)),
                   pl.BlockSpec((None, None, K - 1, C), lambda b, i: (l, b, 0, 0))],
        out_shape=[jax.ShapeDtypeStruct((M, C), BF16),
                   jax.ShapeDtypeStruct((L, B, K - 1, C), F32)],
        scratch_shapes=[pltpu.VMEM((nC, HIST_ROWS + tt, V7X_LANES), F32),
                        pltpu.VMEM((nC, tt, V7X_LANES), F32)],
        input_output_aliases=aliases,
        compiler_params=_cparams(2), name="conv_prompt")(*args)


def _conv_sample_kernel(a_ref, g_ref, cache_ref, w_ref, cb_ref, lg_ref, lb_ref,
                        o_ref, nb_ref, upad_ref, *, T, K, C, bb):
    nC = C // V7X_LANES
    u = _glu(a_ref, g_ref)
    upad_ref[:, 0:K - 1, :] = cache_ref[...]
    upad_ref[:, K - 1:K - 1 + T, :] = u.reshape(bb, T, C)
    nb_ref[...] = upad_ref[:, T:T + K - 1, :]
    chunks = []
    for c in range(nC):
        sl = slice(c * V7X_LANES, (c + 1) * V7X_LANES)
        acc = jnp.zeros((bb, T, V7X_LANES), F32)
        for j in range(K):
            acc = acc + upad_ref[:, j:j + T, sl] * w_ref[j:j + 1, sl]
        chunks.append(acc.reshape(bb * T, V7X_LANES) + cb_ref[:, sl])
    _conv_post(chunks, lg_ref, lb_ref, o_ref, C)


def conv_sample(proj, cache, l, w, cb, lg, lb, row0, B, T, C, K, hc, nb_prev):
    assert T % 8 == 0
    L = cache.shape[0]
    bb = _tile(B, 16, 2)
    assert row0 % (bb * T) == 0
    blk0 = row0 // (bb * T)
    kern = functools.partial(_conv_sample_kernel, T=T, K=K, C=C, bb=bb)
    vec = pl.BlockSpec((1, C), lambda s: (0, 0))
    in_specs = [pl.BlockSpec((bb * T, C), lambda s: (blk0 + s, 0)),
                pl.BlockSpec((bb * T, C), lambda s: (blk0 + s, 1)),
                pl.BlockSpec((None, bb, K - 1, C), lambda s: (l, s, 0, 0)),
                pl.BlockSpec((K, C), lambda s: (0, 0)), vec, vec, vec, ANY]
    in_specs.append(ANY)
    args = [proj, proj, cache, w, cb, lg, lb, hc, nb_prev]
    aliases = {7: 0, 8: 1}
    return pl.pallas_call(
        _skip_refs(kern, 7, 2), grid=(B // bb,), in_specs=in_specs,
        out_specs=[pl.BlockSpec((bb * T, C), lambda s: (blk0 + s, 0)),
                   pl.BlockSpec((None, bb, K - 1, C), lambda s: (l, s, 0, 0))],
        out_shape=[jax.ShapeDtypeStruct(hc.shape, BF16),
                   jax.ShapeDtypeStruct((L, B, K - 1, C), F32)],
        scratch_shapes=[pltpu.VMEM((bb, K - 1 + T, C), F32)],
        input_output_aliases=aliases,
        compiler_params=_cparams(1), name="conv_sample")(*args)


def _retention_tables(H, chunk, rows):
    gamma = (1.0 - 2.0 ** (-5.0 - np.arange(H, dtype=np.float32))).astype(np.float32)
    log_g = np.log(gamma).astype(np.float32)
    idx = np.arange(chunk, dtype=np.float32)
    diff = idx[:, None] - idx[None, :]
    dmask = np.where(diff[None] >= 0, np.exp(diff[None] * log_g[:, None, None]), 0.0).astype(np.float32)
    q_dec = np.exp((idx[None, :] + np.float32(1.0)) * log_g[:, None]).astype(np.float32)
    k_dec = np.exp((np.float32(chunk) - np.float32(1.0) - idx[None, :]) * log_g[:, None]).astype(np.float32)
    c_dec = np.exp(np.float32(chunk) * log_g).astype(np.float32)
    dm = np.zeros((H, rows, rows), np.float32)
    dm[:, :chunk, :chunk] = dmask
    qd = np.zeros((H, rows, 1), np.float32)
    qd[:, :chunk, 0] = q_dec
    kd = np.zeros((H, rows, 1), np.float32)
    kd[:, :chunk, 0] = k_dec
    return dm, qd, kd, [float(c) for c in c_dec]


def _rotary_tables(pos, half):
    inv = (1.0 / (ROPE_BASE ** (np.arange(half, dtype=np.float32) / half))).astype(np.float32)
    ang = (pos.astype(np.float32)[:, None] * inv[None, :]).astype(np.float32)
    return np.cos(ang).astype(np.float32), np.sin(ang).astype(np.float32)


def _retention_kernel(*refs, H, DK, C, nb, cps, rows, c_dec, has_s0):
    if has_s0:
        (q_ref, k_ref, v_ref, g_ref, cos_ref, sin_ref, dm_ref, qd_ref, kd_ref, gng_ref, gnb_ref,
         s0_ref, o_ref, ns_ref) = refs
    else:
        (q_ref, k_ref, v_ref, g_ref, cos_ref, sin_ref, dm_ref, qd_ref, kd_ref, gng_ref, gnb_ref,
         o_ref, ns_ref) = refs
        s0_ref = None
    half = DK // 2
    scale = DK ** -0.5

    @pl.when(pl.program_id(1) == 0)
    def _():
        if has_s0:
            ns_ref[...] = s0_ref[...]
        else:
            ns_ref[...] = jnp.zeros(ns_ref.shape, F32)

    q_all = q_ref[...].astype(F32)
    k_all = k_ref[...].astype(F32)
    v_all = v_ref[...].astype(F32)
    g_all = g_ref[...].astype(F32)

    def rot(x, cos, sin):
        x1, x2 = x[:, :half], x[:, half:]
        return jnp.concatenate([x1 * cos - x2 * sin, x1 * sin + x2 * cos], axis=-1)

    def pad_rows(x):
        if rows == C:
            return x
        return jnp.concatenate([x, jnp.zeros((rows - C, x.shape[1]), x.dtype)], axis=0)

    outs = []
    for n in range(nb * cps):
        s, cc = divmod(n, cps)
        rs = slice(n * C, (n + 1) * C)
        cos = cos_ref[cc * C:(cc + 1) * C, :]
        sin = sin_ref[cc * C:(cc + 1) * C, :]
        heads = []
        for h in range(H):
            hs = slice(h * DK, (h + 1) * DK)
            qr = pad_rows(rot(q_all[rs, hs], cos, sin))
            kr = pad_rows(rot(k_all[rs, hs], cos, sin) * scale)
            vb = pad_rows(v_all[rs, hs]).astype(BF16)
            S = ns_ref[s, h]
            scores = lax.dot_general(qr.astype(BF16), kr.astype(BF16), (((1,), (1,)), ((), ())),
                                     preferred_element_type=F32) * dm_ref[h]
            o = (jnp.dot(scores.astype(BF16), vb, preferred_element_type=F32)
                 + jnp.dot((qr * qd_ref[h]).astype(BF16), S.astype(BF16), preferred_element_type=F32))
            kd = (kr * kd_ref[h]).astype(BF16)
            ns_ref[s, h] = c_dec[h] * S + lax.dot_general(kd, vb, (((0,), (0,)), ((), ())),
                                                          preferred_element_type=F32)
            o = o[:C]
            mu = jnp.mean(o, axis=-1, keepdims=True)
            d = o - mu
            var = jnp.mean(d * d, axis=-1, keepdims=True)
            on = d * lax.rsqrt(var + LN_EPS) * gng_ref[:, hs] + gnb_ref[:, hs]
            heads.append(_silu(g_all[rs, hs]) * on)
        outs.append(jnp.concatenate(heads, axis=-1))
    o_ref[...] = jnp.concatenate(outs, axis=0).astype(o_ref.dtype)


def retention(proj, gng, gnb, row0, B, T, H, DK, pos0, l, hr, ns_prev, state=None):
    M = proj.shape[0]
    L = ns_prev.shape[0]
    R = H * DK
    C = int(np.gcd(T, RET_CHUNK))
    n_chunks = T // C
    nb = 1 if n_chunks > 1 else _tile(B, RET_SEQS_PER_STEP, 1)
    cps = _tile(n_chunks, RET_CHUNKS_PER_STEP, 1)
    n_steps = n_chunks // cps
    rows = max(C, V7X_BF16_SUBLANES)
    dm, qd, kd, c_dec = _retention_tables(H, C, rows)
    cos, sin = _rotary_tables(pos0 + np.arange(T), DK // 2)
    blk_rows = nb * cps * C
    assert row0 % blk_rows == 0 and (blk_rows % V7X_BF16_SUBLANES == 0)
    blk0 = row0 // blk_rows
    has_s0 = state is not None
    kern = functools.partial(_retention_kernel, H=H, DK=DK, C=C, nb=nb, cps=cps, rows=rows,
                             c_dec=c_dec, has_s0=has_s0)

    def row_map(col):
        return lambda sb, c: (blk0 + sb * n_steps + c, col)

    in_specs = [pl.BlockSpec((blk_rows, R), row_map(2)),
                pl.BlockSpec((blk_rows, R), row_map(3)),
                pl.BlockSpec((blk_rows, R), row_map(4)),
                pl.BlockSpec((blk_rows, R), row_map(5)),
                pl.BlockSpec((cps * C, DK // 2), lambda sb, c: (c, 0)),
                pl.BlockSpec((cps * C, DK // 2), lambda sb, c: (c, 0)),
                pl.BlockSpec((H, rows, rows), lambda sb, c: (0, 0, 0)),
                pl.BlockSpec((H, rows, 1), lambda sb, c: (0, 0, 0)),
                pl.BlockSpec((H, rows, 1), lambda sb, c: (0, 0, 0)),
                pl.BlockSpec((1, R), lambda sb, c: (0, 0)),
                pl.BlockSpec((1, R), lambda sb, c: (0, 0))]
    args = [proj, proj, proj, proj, jnp.asarray(cos), jnp.asarray(sin), jnp.asarray(dm),
            jnp.asarray(qd), jnp.asarray(kd), gng, gnb]
    if has_s0:
        in_specs.append(pl.BlockSpec((None, nb, H, DK, DK), lambda sb, c: (l, sb, 0, 0, 0)))
        args.append(state)
    n_used = len(args)
    aliases = {n_used: 0, n_used + 1: 1}
    in_specs += [ANY, ANY]
    args += [hr, ns_prev]
    return pl.pallas_call(
        _skip_refs(kern, n_used, 2), grid=(B // nb, n_steps),
        in_specs=in_specs,
        out_specs=[pl.BlockSpec((blk_rows, R), lambda sb, c: (blk0 + sb * n_steps + c, 0)),
                   pl.BlockSpec((None, nb, H, DK, DK), lambda sb, c: (l, sb, 0, 0, 0))],
        out_shape=[jax.ShapeDtypeStruct((M, R), BF16),
                   jax.ShapeDtypeStruct((L, B, H, DK, DK), F32)],
        input_output_aliases=aliases,
        compiler_params=_cparams(2), name="retention")(*args)


def _outproj_ln_kernel(hc_ref, hr_ref, x_ref, w_ref, g_ref, b_ref, xo_ref, xb_ref, *, alpha, cc, sub):
    for s in range(x_ref.shape[0] // sub):
        rs = slice(s * sub, (s + 1) * sub)
        m = (jnp.dot(hc_ref[rs, :], w_ref[0:cc, :], preferred_element_type=F32)
             + jnp.dot(hr_ref[rs, :], w_ref[cc:, :], preferred_element_type=F32))
        z = _layer_norm(alpha * x_ref[rs, :] + m, g_ref[...], b_ref[...])
        xo_ref[rs, :] = z
        xb_ref[rs, :] = z.astype(BF16)


def outproj_ln(hc, hr, x, wb, g, b, alpha):
    M, D = x.shape
    cc = hc.shape[1]
    tm = _tile(M, 512, V7X_BF16_SUBLANES)
    kern = functools.partial(_outproj_ln_kernel, alpha=alpha, cc=cc, sub=_tile(tm, 128, V7X_BF16_SUBLANES))
    return pl.pallas_call(
        kern, grid=(M // tm,),
        in_specs=[pl.BlockSpec((tm, cc), lambda i: (i, 0)),
                  pl.BlockSpec((tm, hr.shape[1]), lambda i: (i, 0)),
                  pl.BlockSpec((tm, D), lambda i: (i, 0)),
                  pl.BlockSpec(wb.shape, lambda i: (0, 0)),
                  pl.BlockSpec((1, D), lambda i: (0, 0)),
                  pl.BlockSpec((1, D), lambda i: (0, 0))],
        out_specs=[pl.BlockSpec((tm, D), lambda i: (i, 0)),
                   pl.BlockSpec((tm, D), lambda i: (i, 0))],
        out_shape=[jax.ShapeDtypeStruct((M, D), F32), jax.ShapeDtypeStruct((M, D), BF16)],
        compiler_params=_cparams(1), name="outproj_ln")(hc, hr, x, wb, g, b)


def _resid_ln_kernel(f_ref, x_ref, g_ref, b_ref, xo_ref, xb_ref, *, alpha):
    z = _layer_norm(alpha * x_ref[...] + f_ref[...], g_ref[...], b_ref[...])
    xo_ref[...] = z
    xb_ref[...] = z.astype(BF16)


def resid_ln(f, x, g, b, alpha):
    M, D = x.shape
    tm = _tile(M, 512, V7X_BF16_SUBLANES)
    kern = functools.partial(_resid_ln_kernel, alpha=alpha)
    row = pl.BlockSpec((tm, D), lambda i: (i, 0))
    vec = pl.BlockSpec((1, D), lambda i: (0, 0))
    return pl.pallas_call(
        kern, grid=(M // tm,), in_specs=[row, row, vec, vec], out_specs=[row, row],
        out_shape=[jax.ShapeDtypeStruct((M, D), F32), jax.ShapeDtypeStruct((M, D), BF16)],
        compiler_params=_cparams(1), name="resid_ln")(f, x, g, b)


def _tile_state(te_ref, na_ref):
    i = pl.program_id(1)
    active = i < na_ref[0]
    changed = jnp.logical_or(i == 0, te_ref[i] != te_ref[jnp.maximum(i - 1, 0)])
    return active, changed


def _weight_stream(te_ref, na_ref, tend_ref, copies, cast):
    j = pl.program_id(0)
    i = pl.program_id(1)
    active, changed = _tile_state(te_ref, na_ref)

    @pl.when(jnp.logical_and(active, changed))
    def _():
        e = te_ref[i]

        @pl.when(jnp.logical_and(j == 0, i == 0))
        def _():
            for c in copies(e, j):
                c.start()
        for c in copies(e, j):
            c.wait()
        cast()
        nxt = tend_ref[e]
        same_j = nxt < na_ref[0]
        e2 = te_ref[jnp.where(same_j, nxt, 0)]
        j2 = jnp.where(same_j, j, j + 1)

        @pl.when(j2 < pl.num_programs(0))
        def _():
            for c in copies(e2, j2):
                c.start()
    return active


def _row_tile_compute(rows, tm, sub, compute, o_ref):
    @pl.when(rows == tm)
    def _():
        o_ref[...] = compute(0, tm)

    @pl.when(rows < tm)
    def _():
        for s in range(tm // sub):
            @pl.when(s * sub < rows)
            def _():
                o_ref[s * sub:(s + 1) * sub, :] = compute(s * sub, sub)

            @pl.when(s * sub >= rows)
            def _():
                o_ref[s * sub:(s + 1) * sub, :] = jnp.zeros((sub, o_ref.shape[1]), o_ref.dtype)


def _gateup_kernel(te_ref, na_ref, tend_ref, tr_ref, x_ref, wg_hbm, wu_hbm, o_ref,
                   sg_ref, su_ref, wgb_ref, wub_ref, sem, *, li, tn, sub):
    def copies(e, j):
        cols = pl.ds(pl.multiple_of(j * tn, tn), tn)
        return (pltpu.make_async_copy(wg_hbm.at[li, e, :, cols], sg_ref, sem.at[0]),
                pltpu.make_async_copy(wu_hbm.at[li, e, :, cols], su_ref, sem.at[1]))

    def cast():
        wgb_ref[...] = sg_ref[...].astype(BF16)
        wub_ref[...] = su_ref[...].astype(BF16)

    _weight_stream(te_ref, na_ref, tend_ref, copies, cast)

    def compute(r0, n):
        x = x_ref[r0:r0 + n, :]
        gate = jnp.dot(x, wgb_ref[...], preferred_element_type=F32)
        up = jnp.dot(x, wub_ref[...], preferred_element_type=F32)
        return (_silu(gate) * up).astype(o_ref.dtype)

    _row_tile_compute(tr_ref[pl.program_id(1)], x_ref.shape[0], sub, compute, o_ref)


def _down_kernel(te_ref, na_ref, tend_ref, tr_ref, h_ref, w_hbm, o_ref, s_ref, wb_ref, sem, *, li, tn, sub):
    def copies(e, j):
        cols = pl.ds(pl.multiple_of(j * tn, tn), tn)
        return (pltpu.make_async_copy(w_hbm.at[li, e, :, cols], s_ref, sem.at[0]),)

    def cast():
        wb_ref[...] = s_ref[...].astype(BF16)

    _weight_stream(te_ref, na_ref, tend_ref, copies, cast)

    def compute(r0, n):
        return jnp.dot(h_ref[r0:r0 + n, :], wb_ref[...], preferred_element_type=F32)

    _row_tile_compute(tr_ref[pl.program_id(1)], h_ref.shape[0], sub, compute, o_ref)


def grouped_swiglu(xs, wg, wu, wd, li, tile_expert, n_active, tile_end, tile_rows, tm):
    S, D = xs.shape
    Fd = wg.shape[-1]
    nT = S // tm
    tn = _tile(Fd, 512, V7X_LANES)
    sub = _tile(tm, 128, V7X_BF16_SUBLANES)

    def row_map(j, i, te, na, tend, tr):
        return (jnp.minimum(i, na[0] - 1), 0)

    def out_map(j, i, te, na, tend, tr):
        return (i, j)

    h = pl.pallas_call(
        functools.partial(_gateup_kernel, li=li, tn=tn, sub=sub),
        grid_spec=pltpu.PrefetchScalarGridSpec(
            num_scalar_prefetch=4, grid=(Fd // tn, nT),
            in_specs=[pl.BlockSpec((tm, D), row_map), ANY, ANY],
            out_specs=pl.BlockSpec((tm, tn), out_map),
            scratch_shapes=[pltpu.VMEM((D, tn), F32), pltpu.VMEM((D, tn), F32),
                            pltpu.VMEM((D, tn), BF16), pltpu.VMEM((D, tn), BF16),
                            pltpu.SemaphoreType.DMA((2,))]),
        out_shape=jax.ShapeDtypeStruct((S, Fd), BF16),
        compiler_params=_cparams(2), name="ffn_gate_up")(tile_expert, n_active, tile_end, tile_rows, xs, wg, wu)

    tn2 = _tile(D, 512, V7X_LANES)
    return pl.pallas_call(
        functools.partial(_down_kernel, li=li, tn=tn2, sub=sub),
        grid_spec=pltpu.PrefetchScalarGridSpec(
            num_scalar_prefetch=4, grid=(D // tn2, nT),
            in_specs=[pl.BlockSpec((tm, Fd), row_map), ANY],
            out_specs=pl.BlockSpec((tm, tn2), out_map),
            scratch_shapes=[pltpu.VMEM((Fd, tn2), F32), pltpu.VMEM((Fd, tn2), BF16),
                            pltpu.SemaphoreType.DMA((1,))]),
        out_shape=jax.ShapeDtypeStruct((S, D), F32),
        compiler_params=_cparams(2), name="ffn_down")(tile_expert, n_active, tile_end, tile_rows, h, wd)


def _router_kernel(x_ref, w_ref, b_ref, idx_ref, gate_ref):
    logits = jnp.dot(x_ref[...], w_ref[...], precision=lax.Precision.HIGHEST,
                     preferred_element_type=F32) + b_ref[...]
    col = lax.broadcasted_iota(jnp.int32, logits.shape, 1)
    big = jnp.int32(logits.shape[1])
    m1 = jnp.max(logits, axis=-1, keepdims=True)
    i1 = jnp.min(jnp.where(logits == m1, col, big), axis=-1, keepdims=True)
    rest = jnp.where(col == i1, -jnp.inf, logits)
    m2 = jnp.max(rest, axis=-1, keepdims=True)
    i2 = jnp.min(jnp.where(rest == m2, col, big), axis=-1, keepdims=True)
    e = jnp.exp(m2 - m1)
    g1 = 1.0 / (1.0 + e)
    g2 = e / (1.0 + e)
    idx_ref[...] = jnp.where(col == 0, i1, jnp.where(col == 1, i2, 0))
    gate_ref[...] = jnp.where(col == 0, g1, jnp.where(col == 1, g2, 0.0))


def router(x, w_pad, b_pad):
    M, D = x.shape
    tm = _tile(M, 512)
    out = pl.BlockSpec((tm, V7X_LANES), lambda i: (i, 0))
    return pl.pallas_call(
        _router_kernel, grid=(M // tm,),
        in_specs=[pl.BlockSpec((tm, D), lambda i: (i, 0)),
                  pl.BlockSpec((D, V7X_LANES), lambda i: (0, 0)),
                  pl.BlockSpec((1, V7X_LANES), lambda i: (0, 0))],
        out_specs=[out, out],
        out_shape=[jax.ShapeDtypeStruct((M, V7X_LANES), jnp.int32),
                   jax.ShapeDtypeStruct((M, V7X_LANES), F32)],
        compiler_params=_cparams(1), name="router")(x, w_pad, b_pad)


ISSUE_UNROLL = 8


def _dispatch_kernel(src_ref, na_ref, x_hbm, o_ref, buf_ref, sem, *, R):
    i = pl.program_id(0)
    na = na_ref[0]

    def issue_tile(t, slot):
        def body(r, carry):
            tok = src_ref[t * R + r]
            pltpu.make_async_copy(x_hbm.at[pl.ds(tok, 1), :], buf_ref.at[slot, pl.ds(r, 1), :],
                                  sem.at[slot]).start()
            return carry
        lax.fori_loop(0, R, body, 0, unroll=ISSUE_UNROLL)

    @pl.when(i == 0)
    def _():
        issue_tile(0, 0)

    @pl.when(i + 1 < na)
    def _():
        issue_tile(i + 1, (i + 1) % 2)

    @pl.when(i < na)
    def _():
        slot = i % 2
        pltpu.make_async_copy(x_hbm.at[pl.ds(0, R), :], buf_ref.at[slot], sem.at[slot]).wait()
        o_ref[...] = buf_ref[slot].astype(o_ref.dtype)

    @pl.when(i >= na)
    def _():
        o_ref[...] = jnp.zeros(o_ref.shape, o_ref.dtype)


def dispatch(x, src, n_active, S, R):
    M, D = x.shape
    kern = functools.partial(_dispatch_kernel, R=R)
    return pl.pallas_call(
        kern,
        grid_spec=pltpu.PrefetchScalarGridSpec(
            num_scalar_prefetch=2, grid=(S // R,),
            in_specs=[ANY],
            out_specs=pl.BlockSpec((R, D), lambda i, src, na: (i, 0)),
            scratch_shapes=[pltpu.VMEM((2, R, D), F32), pltpu.SemaphoreType.DMA((2,))]),
        out_shape=jax.ShapeDtypeStruct((S, D), BF16),
        compiler_params=_cparams(1), name="moe_dispatch")(src, n_active, x)


def _combine_ln_kernel(pos_ref, ys_hbm, x_ref, gate_ref, g_ref, b_ref, xo_ref, xb_ref, buf_ref, sem,
                       *, R, alpha):
    i = pl.program_id(0)

    def issue_tile(t, slot):
        def body(r, carry):
            a = (t * R + r) * TOP_K
            for k in range(TOP_K):
                pltpu.make_async_copy(ys_hbm.at[pl.ds(pos_ref[a + k], 1), :],
                                      buf_ref.at[slot, k, pl.ds(r, 1), :], sem.at[slot]).start()
            return carry
        lax.fori_loop(0, R, body, 0, unroll=ISSUE_UNROLL // TOP_K)

    @pl.when(i == 0)
    def _():
        issue_tile(0, 0)

    @pl.when(i + 1 < pl.num_programs(0))
    def _():
        issue_tile(i + 1, (i + 1) % 2)

    slot = i % 2
    for k in range(TOP_K):
        pltpu.make_async_copy(ys_hbm.at[pl.ds(0, R), :], buf_ref.at[slot, k], sem.at[slot]).wait()
    gates = gate_ref[...]
    y = gates[:, 0:1] * buf_ref[slot, 0]
    for k in range(1, TOP_K):
        y = y + gates[:, k:k + 1] * buf_ref[slot, k]
    z = _layer_norm(alpha * x_ref[...] + y, g_ref[...], b_ref[...])
    xo_ref[...] = z
    xb_ref[...] = z.astype(BF16)


def combine_ln(ys, pos, x, gates, g, b, alpha):
    M, D = x.shape
    R = _tile(M, 256, V7X_BF16_SUBLANES)
    kern = functools.partial(_combine_ln_kernel, R=R, alpha=alpha)
    row = lambda i, pos: (i, 0)
    fix = lambda i, pos: (0, 0)
    return pl.pallas_call(
        kern,
        grid_spec=pltpu.PrefetchScalarGridSpec(
            num_scalar_prefetch=1, grid=(M // R,),
            in_specs=[ANY,
                      pl.BlockSpec((R, D), row),
                      pl.BlockSpec((R, V7X_LANES), row),
                      pl.BlockSpec((1, D), fix),
                      pl.BlockSpec((1, D), fix)],
            out_specs=[pl.BlockSpec((R, D), row), pl.BlockSpec((R, D), row)],
            scratch_shapes=[pltpu.VMEM((2, TOP_K, R, D), F32), pltpu.SemaphoreType.DMA((2,))]),
        out_shape=[jax.ShapeDtypeStruct((M, D), F32), jax.ShapeDtypeStruct((M, D), BF16)],
        compiler_params=_cparams(1), name="moe_combine_ln")(pos, ys, x, gates, g, b)


def _routing_plan(idx, E, tm, n_tiles):
    e_flat = idx.reshape(-1)
    A = e_flat.shape[0]
    a_ids = jnp.arange(A, dtype=jnp.int32)
    order = jnp.sort(e_flat * A + a_ids) % A
    rank = jnp.argsort(order).astype(jnp.int32)
    counts = jnp.sum((e_flat[:, None] == jnp.arange(E, dtype=jnp.int32)[None, :]).astype(jnp.int32), axis=0)
    tiles_e = (counts + tm - 1) // tm
    tile_end = jnp.cumsum(tiles_e).astype(jnp.int32)
    tile_start = tile_end - tiles_e
    pad_start = tile_start * tm
    start = jnp.cumsum(counts) - counts
    n_active = tile_end[-1]
    pos = pad_start[e_flat] + rank - start[e_flat]
    tile_id = jnp.arange(n_tiles, dtype=jnp.int32)
    te = jnp.searchsorted(tile_end, jnp.minimum(tile_id, n_active - 1), side="right").astype(jnp.int32)
    tile_rows = jnp.where(tile_id < n_active,
                          jnp.clip(counts[te] - (tile_id - tile_start[te]) * tm, 0, tm), 0)
    slot = jnp.arange(n_tiles * tm, dtype=jnp.int32)
    se = te[slot // tm]
    off = slot - pad_start[se]
    r = start[se] + jnp.clip(off, 0, jnp.maximum(counts[se] - 1, 0))
    src = order[jnp.clip(r, 0, A - 1)] // TOP_K
    return (src.astype(jnp.int32), pos.astype(jnp.int32), te,
            n_active.reshape(1).astype(jnp.int32), tile_end, tile_rows.astype(jnp.int32))


def kernel(x_prompt, x_sample, cache_conv, state_ret, w_in, conv_w, conv_b, conv_ln_g, conv_ln_b,
           ret_gn_g, ret_gn_b, w_out, ln1_g, ln1_b, ln2_g, ln2_b, w_ff_gate, w_ff_up, w_ff_down,
           w_router, b_router, w_exp_gate, w_exp_up, w_exp_down):
    B, T, D = x_prompt.shape
    Bs, Ts, _ = x_sample.shape
    L = w_in.shape[0]
    K = conv_w.shape[1]
    C = conv_w.shape[2]
    H, DK = state_ret.shape[2], state_ret.shape[3]
    E = w_router.shape[-1]
    Np, Ns = B * T, Bs * Ts
    M = Np + Ns
    nC = C // V7X_LANES
    alpha = float((2 * L) ** 0.25)
    assert w_in.shape[-1] == 6 * C and H * DK == C and D == 2 * C

    x = jnp.concatenate([x_prompt.reshape(Np, D), x_sample.reshape(Ns, D)], axis=0)
    xb = x.astype(BF16)

    tm_dense = _tile(M, 1024, V7X_BF16_SUBLANES)
    n_dense_tiles = M // tm_dense
    dense_te = jnp.zeros((n_dense_tiles,), jnp.int32)
    dense_na = jnp.full((1,), n_dense_tiles, jnp.int32)
    dense_rows = jnp.full((n_dense_tiles,), tm_dense, jnp.int32)
    w_ff_gate4, w_ff_up4, w_ff_down4 = w_ff_gate[:, None], w_ff_up[:, None], w_ff_down[:, None]
    moe_tm = _tile(M * TOP_K, MOE_TILE, V7X_BF16_SUBLANES)
    moe_tiles = (M * TOP_K) // moe_tm + E

    hc = jnp.zeros((M, C), BF16)
    hr = jnp.zeros((M, H * DK), BF16)
    conv_p = jnp.zeros((L, B, K - 1, C), F32)
    conv_s = jnp.zeros((L, Bs, K - 1, C), F32)
    ret_p = jnp.zeros((L, B, H, DK, DK), F32)
    ret_s = jnp.zeros((L, Bs, H, DK, DK), F32)
    for l in range(L):
        proj = matmul_ws(xb, w_in, l, BF16)

        w3 = conv_w[l].reshape(K, nC, V7X_LANES).transpose(1, 0, 2)
        cb3 = conv_b[l].reshape(nC, 1, V7X_LANES)
        cb, lg, lb = conv_b[l][None], conv_ln_g[l][None], conv_ln_b[l][None]
        hc, conv_p = conv_prompt(proj, w3, cb3, lg, lb, B, T, C, K, l, hc, conv_p)
        hc, conv_s = conv_sample(proj, cache_conv, l, conv_w[l], cb, lg, lb, Np, Bs, Ts, C, K, hc, conv_s)

        gng, gnb = ret_gn_g[l][None], ret_gn_b[l][None]
        hr, ret_p = retention(proj, gng, gnb, 0, B, T, H, DK, 0, l, hr, ret_p)
        hr, ret_s = retention(proj, gng, gnb, Np, Bs, Ts, H, DK, PAST_LEN, l, hr, ret_s, state=state_ret)

        x, xb = outproj_ln(hc, hr, x, cast_weight(w_out, l), ln1_g[l][None], ln1_b[l][None], alpha)

        if l % 2 == 0:
            f = grouped_swiglu(xb, w_ff_gate4, w_ff_up4, w_ff_down4, l // 2, dense_te, dense_na,
                               dense_na, dense_rows, tm_dense)
            x, xb = resid_ln(f, x, ln2_g[l][None], ln2_b[l][None], alpha)
        else:
            li = l // 2
            w_pad = jnp.zeros((D, V7X_LANES), F32).at[:, :E].set(w_router[li])
            b_pad = jnp.full((1, V7X_LANES), -1e30, F32).at[0, :E].set(b_router[li])
            idx_w, gate_w = router(x, w_pad, b_pad)
            src, pos, te, na, tend, trows = _routing_plan(idx_w[:, :TOP_K], E, moe_tm, moe_tiles)
            xs = dispatch(x, src, na, moe_tiles * moe_tm, moe_tm)
            ys = grouped_swiglu(xs, w_exp_gate, w_exp_up, w_exp_down, li, te, na, tend, trows, moe_tm)
            x, xb = combine_ln(ys, pos, x, gate_w, ln2_g[l][None], ln2_b[l][None], alpha)

    y_prompt = x[:Np].reshape(B, T, D)
    y_sample = x[Np:].reshape(Bs, Ts, D)
    return (y_prompt, y_sample, conv_p, ret_p, conv_s, ret_s)
```

```python
import functools

import numpy as np
import jax
import jax.numpy as jnp
from jax import lax
from jax.experimental import pallas as pl
from jax.experimental.pallas import tpu as pltpu

F32 = jnp.float32
BF16 = jnp.bfloat16

LN_EPS = 1e-5
ROPE_BASE = 10000.0
RET_CHUNK = 128
PAST_LEN = 16384
TOP_K = 2

V7X_LANES = 128
V7X_BF16_SUBLANES = 16
V7X_VMEM_LIMIT_BYTES = 56 * 1024 * 1024

HIST_ROWS = 32
MOE_TILE = 1024
RET_SEQS_PER_STEP = 4
RET_CHUNKS_PER_STEP = 4
ANY = pl.BlockSpec(memory_space=pl.ANY)


def _cparams(n_axes):
    return pltpu.CompilerParams(dimension_semantics=("arbitrary",) * n_axes,
                                vmem_limit_bytes=V7X_VMEM_LIMIT_BYTES)


def _tile(n, pref, mult=8):
    t = min(pref, n)
    while t > mult and (n % t or t % mult):
        t -= mult
    assert n % t == 0 and t % mult == 0, (n, pref, mult)
    return t


def _skip_refs(kern, start, count):
    def wrapped(*refs):
        return kern(*refs[:start], *refs[start + count:])
    return wrapped


def _layer_norm(y, g, b):
    mu = jnp.mean(y, axis=-1, keepdims=True)
    d = y - mu
    var = jnp.mean(d * d, axis=-1, keepdims=True)
    return d * lax.rsqrt(var + LN_EPS) * g + b


def _silu(z):
    return z * jax.nn.sigmoid(z)


def _cast_kernel(w_ref, o_ref):
    o_ref[...] = w_ref[...].astype(o_ref.dtype)


def cast_weight(w, l):
    _, K, N = w.shape
    tr = _tile(K, 512, V7X_BF16_SUBLANES)
    return pl.pallas_call(
        _cast_kernel, grid=(K // tr,),
        in_specs=[pl.BlockSpec((None, tr, N), lambda i: (l, i, 0))],
        out_specs=pl.BlockSpec((tr, N), lambda i: (i, 0)),
        out_shape=jax.ShapeDtypeStruct((K, N), BF16),
        compiler_params=_cparams(1), name="cast_weight")(w)


def _pack_tokens_kernel(xp_ref, xs_ref, x_ref, xb_ref, *, split_tile):
    @pl.when(pl.program_id(0) < split_tile)
    def _():
        x_ref[...] = xp_ref[...]
        xb_ref[...] = xp_ref[...].astype(BF16)

    @pl.when(pl.program_id(0) >= split_tile)
    def _():
        x_ref[...] = xs_ref[...]
        xb_ref[...] = xs_ref[...].astype(BF16)


def pack_tokens(xp, xs):
    (Np, D), Ns = xp.shape, xs.shape[0]
    tm = _tile(int(np.gcd(Np, Ns)), 512, V7X_BF16_SUBLANES)
    st = Np // tm
    out = pl.BlockSpec((tm, D), lambda i: (i, 0))
    return pl.pallas_call(
        functools.partial(_pack_tokens_kernel, split_tile=st), grid=((Np + Ns) // tm,),
        in_specs=[pl.BlockSpec((tm, D), lambda i: (jnp.minimum(i, st - 1), 0)),
                  pl.BlockSpec((tm, D), lambda i: (jnp.maximum(i - st, 0), 0))],
        out_specs=[out, out],
        out_shape=[jax.ShapeDtypeStruct((Np + Ns, D), F32), jax.ShapeDtypeStruct((Np + Ns, D), BF16)],
        compiler_params=_cparams(1), name="pack_tokens")(xp, xs)


def _matmul_ws_kernel(x_ref, w_ref, o_ref, wb_ref):
    @pl.when(pl.program_id(1) == 0)
    def _():
        wb_ref[...] = w_ref[...].astype(BF16)
    o_ref[...] = jnp.dot(x_ref[...], wb_ref[...], preferred_element_type=F32).astype(o_ref.dtype)


def matmul_ws(x, w, l, out_dtype):
    M, K = x.shape
    N = w.shape[-1]
    tm = _tile(M, 1024, V7X_BF16_SUBLANES)
    tn = _tile(N, 1024, V7X_LANES)
    return pl.pallas_call(
        _matmul_ws_kernel, grid=(N // tn, M // tm),
        in_specs=[pl.BlockSpec((tm, K), lambda j, i: (i, 0)),
                  pl.BlockSpec((None, K, tn), lambda j, i: (l, 0, j))],
        out_specs=pl.BlockSpec((tm, tn), lambda j, i: (i, j)),
        out_shape=jax.ShapeDtypeStruct((M, N), out_dtype),
        scratch_shapes=[pltpu.VMEM((K, tn), BF16)],
        compiler_params=_cparams(2), name="in_proj")(x, w)


def _conv_post(conv_chunks, lg_ref, lb_ref, o_ref, C):
    s = conv_chunks[0]
    for ch in conv_chunks[1:]:
        s = s + ch
    mu = jnp.sum(s, axis=-1, keepdims=True) / C
    sq = None
    for ch in conv_chunks:
        d = ch - mu
        sq = d * d if sq is None else sq + d * d
    var = jnp.sum(sq, axis=-1, keepdims=True) / C
    inv = lax.rsqrt(var + LN_EPS)
    for c, ch in enumerate(conv_chunks):
        sl = slice(c * V7X_LANES, (c + 1) * V7X_LANES)
        z = (ch - mu) * inv * lg_ref[:, sl] + lb_ref[:, sl]
        o_ref[:, sl] = _silu(z).astype(o_ref.dtype)


def _glu(a_ref, g_ref):
    return a_ref[...].astype(F32) * jax.nn.sigmoid(g_ref[...].astype(F32))


def _conv_prompt_kernel(a_ref, g_ref, ha_ref, hg_ref, w3_ref, cb3_ref, lg_ref, lb_ref,
                        o_ref, nb_ref, upad_ref, conv_ref, *, tt, K, C, rb):
    i = pl.program_id(1)
    nC = C // V7X_LANES
    off = HIST_ROWS - (K - 1)
    u = _glu(a_ref, g_ref)
    uh = jnp.where(i > 0, _glu(ha_ref, hg_ref), 0.0)
    for c in range(nC):
        sl = slice(c * V7X_LANES, (c + 1) * V7X_LANES)
        upad_ref[c, 0:HIST_ROWS, :] = uh[:, sl]
        upad_ref[c, HIST_ROWS:HIST_ROWS + tt, :] = u[:, sl]

    def chunk_body(c, carry):
        for r in range(tt // rb):
            acc = jnp.zeros((rb, V7X_LANES), F32)
            for j in range(K):
                lo = r * rb + off + j
                acc = acc + upad_ref[c, lo:lo + rb, :] * w3_ref[c, j:j + 1, :]
            conv_ref[c, r * rb:(r + 1) * rb, :] = acc + cb3_ref[c]
        return carry

    lax.fori_loop(0, nC, chunk_body, 0)
    _conv_post([conv_ref[c] for c in range(nC)], lg_ref, lb_ref, o_ref, C)

    @pl.when(i == pl.num_programs(1) - 1)
    def _():
        for c in range(nC):
            sl = slice(c * V7X_LANES, (c + 1) * V7X_LANES)
            nb_ref[:, sl] = upad_ref[c, HIST_ROWS + tt - (K - 1):HIST_ROWS + tt, :]


def conv_prompt(proj, w3, cb3, lg, lb, B, T, C, K, l, hc, nb_prev):
    M = proj.shape[0]
    L = nb_prev.shape[0]
    tt = _tile(T, 256, HIST_ROWS)
    nT = T // tt
    nC = C // V7X_LANES
    hb = tt // HIST_ROWS
    kern = functools.partial(_conv_prompt_kernel, tt=tt, K=K, C=C, rb=_tile(tt, 64))
    hist = lambda col: (lambda b, i: (jnp.maximum((b * nT + i) * hb - 1, 0), col))
    in_specs = [pl.BlockSpec((tt, C), lambda b, i: (b * nT + i, 0)),
                pl.BlockSpec((tt, C), lambda b, i: (b * nT + i, 1)),
                pl.BlockSpec((HIST_ROWS, C), hist(0)),
                pl.BlockSpec((HIST_ROWS, C), hist(1)),
                pl.BlockSpec((nC, K, V7X_LANES), lambda b, i: (0, 0, 0)),
                pl.BlockSpec((nC, 1, V7X_LANES), lambda b, i: (0, 0, 0)),
                pl.BlockSpec((1, C), lambda b, i: (0, 0)),
                pl.BlockSpec((1, C), lambda b, i: (0, 0))]
    args = [proj, proj, proj, proj, w3, cb3, lg, lb, hc, nb_prev]
    in_specs += [ANY, ANY]
    aliases = {8: 0, 9: 1}
    return pl.pallas_call(
        _skip_refs(kern, 8, 2), grid=(B, nT), in_specs=in_specs,
        out_specs=[pl.BlockSpec((tt, C), lambda b, i: (b * nT + i, 0)),
                   pl.BlockSpec((None, None, K - 1, C), lambda b, i: (l, b, 0, 0))],
        out_shape=[jax.ShapeDtypeStruct((M, C), BF16),
                   jax.ShapeDtypeStruct((L, B, K - 1, C), F32)],
        scratch_shapes=[pltpu.VMEM((nC, HIST_ROWS + tt, V7X_LANES), F32),
                        pltpu.VMEM((nC, tt, V7X_LANES), F32)],
        input_output_aliases=aliases,
        compiler_params=_cparams(2), name="conv_prompt")(*args)


def _conv_sample_kernel(a_ref, g_ref, cache_ref, w_ref, cb_ref, lg_ref, lb_ref,
                        o_ref, nb_ref, upad_ref, *, T, K, C, bb):
    nC = C // V7X_LANES
    u = _glu(a_ref, g_ref)
    upad_ref[:, 0:K - 1, :] = cache_ref[...]
    upad_ref[:, K - 1:K - 1 + T, :] = u.reshape(bb, T, C)
    nb_ref[...] = upad_ref[:, T:T + K - 1, :]
    chunks = []
    for c in range(nC):
        sl = slice(c * V7X_LANES, (c + 1) * V7X_LANES)
        acc = jnp.zeros((bb, T, V7X_LANES), F32)
        for j in range(K):
            acc = acc + upad_ref[:, j:j + T, sl] * w_ref[j:j + 1, sl]
        chunks.append(acc.reshape(bb * T, V7X_LANES) + cb_ref[:, sl])
    _conv_post(chunks, lg_ref, lb_ref, o_ref, C)


def conv_sample(proj, cache, l, w, cb, lg, lb, row0, B, T, C, K, hc, nb_prev):
    assert T % 8 == 0
    L = cache.shape[0]
    bb = _tile(B, 16, 2)
    assert row0 % (bb * T) == 0
    blk0 = row0 // (bb * T)
    kern = functools.partial(_conv_sample_kernel, T=T, K=K, C=C, bb=bb)
    vec = pl.BlockSpec((1, C), lambda s: (0, 0))
    in_specs = [pl.BlockSpec((bb * T, C), lambda s: (blk0 + s, 0)),
                pl.BlockSpec((bb * T, C), lambda s: (blk0 + s, 1)),
                pl.BlockSpec((None, bb, K - 1, C), lambda s: (l, s, 0, 0)),
                pl.BlockSpec((K, C), lambda s: (0, 0)), vec, vec, vec, ANY]
    in_specs.append(ANY)
    args = [proj, proj, cache, w, cb, lg, lb, hc, nb_prev]
    aliases = {7: 0, 8: 1}
    return pl.pallas_call(
        _skip_refs(kern, 7, 2), grid=(B // bb,), in_specs=in_specs,
        out_specs=[pl.BlockSpec((bb * T, C), lambda s: (blk0 + s, 0)),
                   pl.BlockSpec((None, bb, K - 1, C), lambda s: (l, s, 0, 0))],
        out_shape=[jax.ShapeDtypeStruct(hc.shape, BF16),
                   jax.ShapeDtypeStruct((L, B, K - 1, C), F32)],
        scratch_shapes=[pltpu.VMEM((bb, K - 1 + T, C), F32)],
        input_output_aliases=aliases,
        compiler_params=_cparams(1), name="conv_sample")(*args)


def _retention_tables(H, chunk, rows):
    gamma = (1.0 - 2.0 ** (-5.0 - np.arange(H, dtype=np.float32))).astype(np.float32)
    log_g = np.log(gamma).astype(np.float32)
    idx = np.arange(chunk, dtype=np.float32)
    diff = idx[:, None] - idx[None, :]
    dmask = np.where(diff[None] >= 0, np.exp(diff[None] * log_g[:, None, None]), 0.0).astype(np.float32)
    q_dec = np.exp((idx[None, :] + np.float32(1.0)) * log_g[:, None]).astype(np.float32)
    k_dec = np.exp((np.float32(chunk) - np.float32(1.0) - idx[None, :]) * log_g[:, None]).astype(np.float32)
    c_dec = np.exp(np.float32(chunk) * log_g).astype(np.float32)
    dm = np.zeros((H, rows, rows), np.float32)
    dm[:, :chunk, :chunk] = dmask
    qd = np.zeros((H, rows, 1), np.float32)
    qd[:, :chunk, 0] = q_dec
    kd = np.zeros((H, rows, 1), np.float32)
    kd[:, :chunk, 0] = k_dec
    return dm, qd, kd, [float(c) for c in c_dec]


def _rotary_tables(pos, half):
    inv = (1.0 / (ROPE_BASE ** (np.arange(half, dtype=np.float32) / half))).astype(np.float32)
    ang = (pos.astype(np.float32)[:, None] * inv[None, :]).astype(np.float32)
    return np.cos(ang).astype(np.float32), np.sin(ang).astype(np.float32)


def _retention_kernel(*refs, H, DK, C, nb, cps, rows, c_dec, has_s0):
    if has_s0:
        (q_ref, k_ref, v_ref, g_ref, cos_ref, sin_ref, dm_ref, qd_ref, kd_ref, gng_ref, gnb_ref,
         s0_ref, o_ref, ns_ref) = refs
    else:
        (q_ref, k_ref, v_ref, g_ref, cos_ref, sin_ref, dm_ref, qd_ref, kd_ref, gng_ref, gnb_ref,
         o_ref, ns_ref) = refs
        s0_ref = None
    half = DK // 2
    scale = DK ** -0.5

    @pl.when(pl.program_id(1) == 0)
    def _():
        if has_s0:
            ns_ref[...] = s0_ref[...]
        else:
            ns_ref[...] = jnp.zeros(ns_ref.shape, F32)

    q_all = q_ref[...].astype(F32)
    k_all = k_ref[...].astype(F32)
    v_all = v_ref[...].astype(F32)
    g_all = g_ref[...].astype(F32)

    def rot(x, cos, sin):
        x1, x2 = x[:, :half], x[:, half:]
        return jnp.concatenate([x1 * cos - x2 * sin, x1 * sin + x2 * cos], axis=-1)

    def pad_rows(x):
        if rows == C:
            return x
        return jnp.concatenate([x, jnp.zeros((rows - C, x.shape[1]), x.dtype)], axis=0)

    outs = []
    for n in range(nb * cps):
        s, cc = divmod(n, cps)
        rs = slice(n * C, (n + 1) * C)
        cos = cos_ref[cc * C:(cc + 1) * C, :]
        sin = sin_ref[cc * C:(cc + 1) * C, :]
        heads = []
        for h in range(H):
            hs = slice(h * DK, (h + 1) * DK)
            qr = pad_rows(rot(q_all[rs, hs], cos, sin))
            kr = pad_rows(rot(k_all[rs, hs], cos, sin) * scale)
            vb = pad_rows(v_all[rs, hs]).astype(BF16)
            S = ns_ref[s, h]
            scores = lax.dot_general(qr.astype(BF16), kr.astype(BF16), (((1,), (1,)), ((), ())),
                                     preferred_element_type=F32) * dm_ref[h]
            o = (jnp.dot(scores.astype(BF16), vb, preferred_element_type=F32)
                 + jnp.dot((qr * qd_ref[h]).astype(BF16), S.astype(BF16), preferred_element_type=F32))
            kd = (kr * kd_ref[h]).astype(BF16)
            ns_ref[s, h] = c_dec[h] * S + lax.dot_general(kd, vb, (((0,), (0,)), ((), ())),
                                                          preferred_element_type=F32)
            o = o[:C]
            mu = jnp.mean(o, axis=-1, keepdims=True)
            d = o - mu
            var = jnp.mean(d * d, axis=-1, keepdims=True)
            on = d * lax.rsqrt(var + LN_EPS) * gng_ref[:, hs] + gnb_ref[:, hs]
            heads.append(_silu(g_all[rs, hs]) * on)
        outs.append(jnp.concatenate(heads, axis=-1))
    o_ref[...] = jnp.concatenate(outs, axis=0).astype(o_ref.dtype)


def retention(proj, gng, gnb, row0, B, T, H, DK, pos0, l, hr, ns_prev, state=None):
    M = proj.shape[0]
    L = ns_prev.shape[0]
    R = H * DK
    C = int(np.gcd(T, RET_CHUNK))
    n_chunks = T // C
    nb = 1 if n_chunks > 1 else _tile(B, RET_SEQS_PER_STEP, 1)
    cps = _tile(n_chunks, RET_CHUNKS_PER_STEP, 1)
    n_steps = n_chunks // cps
    rows = max(C, V7X_BF16_SUBLANES)
    dm, qd, kd, c_dec = _retention_tables(H, C, rows)
    cos, sin = _rotary_tables(pos0 + np.arange(T), DK // 2)
    blk_rows = nb * cps * C
    assert row0 % blk_rows == 0 and (blk_rows % V7X_BF16_SUBLANES == 0)
    blk0 = row0 // blk_rows
    has_s0 = state is not None
    kern = functools.partial(_retention_kernel, H=H, DK=DK, C=C, nb=nb, cps=cps, rows=rows,
                             c_dec=c_dec, has_s0=has_s0)

    def row_map(col):
        return lambda sb, c: (blk0 + sb * n_steps + c, col)

    in_specs = [pl.BlockSpec((blk_rows, R), row_map(2)),
                pl.BlockSpec((blk_rows, R), row_map(3)),
                pl.BlockSpec((blk_rows, R), row_map(4)),
                pl.BlockSpec((blk_rows, R), row_map(5)),
                pl.BlockSpec((cps * C, DK // 2), lambda sb, c: (c, 0)),
                pl.BlockSpec((cps * C, DK // 2), lambda sb, c: (c, 0)),
                pl.BlockSpec((H, rows, rows), lambda sb, c: (0, 0, 0)),
                pl.BlockSpec((H, rows, 1), lambda sb, c: (0, 0, 0)),
                pl.BlockSpec((H, rows, 1), lambda sb, c: (0, 0, 0)),
                pl.BlockSpec((1, R), lambda sb, c: (0, 0)),
                pl.BlockSpec((1, R), lambda sb, c: (0, 0))]
    args = [proj, proj, proj, proj, jnp.asarray(cos), jnp.asarray(sin), jnp.asarray(dm),
            jnp.asarray(qd), jnp.asarray(kd), gng, gnb]
    if has_s0:
        in_specs.append(pl.BlockSpec((None, nb, H, DK, DK), lambda sb, c: (l, sb, 0, 0, 0)))
        args.append(state)
    n_used = len(args)
    aliases = {n_used: 0, n_used + 1: 1}
    in_specs += [ANY, ANY]
    args += [hr, ns_prev]
    return pl.pallas_call(
        _skip_refs(kern, n_used, 2), grid=(B // nb, n_steps),
        in_specs=in_specs,
        out_specs=[pl.BlockSpec((blk_rows, R), lambda sb, c: (blk0 + sb * n_steps + c, 0)),
                   pl.BlockSpec((None, nb, H, DK, DK), lambda sb, c: (l, sb, 0, 0, 0))],
        out_shape=[jax.ShapeDtypeStruct((M, R), BF16),
                   jax.ShapeDtypeStruct((L, B, H, DK, DK), F32)],
        input_output_aliases=aliases,
        compiler_params=_cparams(2), name="retention")(*args)


def _outproj_ln_kernel(hc_ref, hr_ref, x_ref, w_ref, g_ref, b_ref, xo_ref, xb_ref, *, alpha, cc, sub):
    for s in range(x_ref.shape[0] // sub):
        rs = slice(s * sub, (s + 1) * sub)
        m = (jnp.dot(hc_ref[rs, :], w_ref[0:cc, :], preferred_element_type=F32)
             + jnp.dot(hr_ref[rs, :], w_ref[cc:, :], preferred_element_type=F32))
        z = _layer_norm(alpha * x_ref[rs, :] + m, g_ref[...], b_ref[...])
        xo_ref[rs, :] = z
        xb_ref[rs, :] = z.astype(BF16)


def outproj_ln(hc, hr, x, wb, g, b, alpha):
    M, D = x.shape
    cc = hc.shape[1]
    tm = _tile(M, 512, V7X_BF16_SUBLANES)
    kern = functools.partial(_outproj_ln_kernel, alpha=alpha, cc=cc, sub=_tile(tm, 128, V7X_BF16_SUBLANES))
    return pl.pallas_call(
        kern, grid=(M // tm,),
        in_specs=[pl.BlockSpec((tm, cc), lambda i: (i, 0)),
                  pl.BlockSpec((tm, hr.shape[1]), lambda i: (i, 0)),
                  pl.BlockSpec((tm, D), lambda i: (i, 0)),
                  pl.BlockSpec(wb.shape, lambda i: (0, 0)),
                  pl.BlockSpec((1, D), lambda i: (0, 0)),
                  pl.BlockSpec((1, D), lambda i: (0, 0))],
        out_specs=[pl.BlockSpec((tm, D), lambda i: (i, 0)),
                   pl.BlockSpec((tm, D), lambda i: (i, 0))],
        out_shape=[jax.ShapeDtypeStruct((M, D), F32), jax.ShapeDtypeStruct((M, D), BF16)],
        compiler_params=_cparams(1), name="outproj_ln")(hc, hr, x, wb, g, b)


def _ln_row_tile(M, n_first, pref):
    n = M if n_first is None else int(np.gcd(n_first, M - n_first))
    return _tile(n, pref, V7X_BF16_SUBLANES)


def _ln_outputs(M, D, tm, n_first, wrap):
    if n_first is None:
        row = pl.BlockSpec((tm, D), wrap(lambda i: (i, 0)))
        return [row, row], [jax.ShapeDtypeStruct((M, D), F32), jax.ShapeDtypeStruct((M, D), BF16)], None
    st = n_first // tm
    specs = [pl.BlockSpec((tm, D), wrap(lambda i: (jnp.minimum(i, st - 1), 0))),
             pl.BlockSpec((tm, D), wrap(lambda i: (jnp.maximum(i - st, 0), 0)))]
    return specs, [jax.ShapeDtypeStruct((n_first, D), F32), jax.ShapeDtypeStruct((M - n_first, D), F32)], st


def _ln_emit(z, o0_ref, o1_ref, split_tile):
    if split_tile is None:
        o0_ref[...] = z
        o1_ref[...] = z.astype(BF16)
    else:
        @pl.when(pl.program_id(0) < split_tile)
        def _():
            o0_ref[...] = z

        @pl.when(pl.program_id(0) >= split_tile)
        def _():
            o1_ref[...] = z


def _resid_ln_kernel(f_ref, x_ref, g_ref, b_ref, o0_ref, o1_ref, *, alpha, split_tile):
    z = _layer_norm(alpha * x_ref[...] + f_ref[...], g_ref[...], b_ref[...])
    _ln_emit(z, o0_ref, o1_ref, split_tile)


def resid_ln(f, x, g, b, alpha, n_first=None):
    M, D = x.shape
    tm = _ln_row_tile(M, n_first, 512)
    out_specs, out_shape, st = _ln_outputs(M, D, tm, n_first, lambda fn: fn)
    kern = functools.partial(_resid_ln_kernel, alpha=alpha, split_tile=st)
    row = pl.BlockSpec((tm, D), lambda i: (i, 0))
    vec = pl.BlockSpec((1, D), lambda i: (0, 0))
    return pl.pallas_call(
        kern, grid=(M // tm,), in_specs=[row, row, vec, vec], out_specs=out_specs, out_shape=out_shape,
        compiler_params=_cparams(1), name="resid_ln")(f, x, g, b)


def _tile_state(te_ref, na_ref):
    i = pl.program_id(1)
    active = i < na_ref[0]
    changed = jnp.logical_or(i == 0, te_ref[i] != te_ref[jnp.maximum(i - 1, 0)])
    return active, changed


def _weight_stream(te_ref, na_ref, tend_ref, copies, cast):
    j = pl.program_id(0)
    i = pl.program_id(1)
    active, changed = _tile_state(te_ref, na_ref)

    @pl.when(jnp.logical_and(active, changed))
    def _():
        e = te_ref[i]

        @pl.when(jnp.logical_and(j == 0, i == 0))
        def _():
            for c in copies(e, j):
                c.start()
        for c in copies(e, j):
            c.wait()
        cast()
        nxt = tend_ref[e]
        same_j = nxt < na_ref[0]
        e2 = te_ref[jnp.where(same_j, nxt, 0)]
        j2 = jnp.where(same_j, j, j + 1)

        @pl.when(j2 < pl.num_programs(0))
        def _():
            for c in copies(e2, j2):
                c.start()
    return active


def _row_tile_compute(rows, tm, sub, compute, o_ref):
    @pl.when(rows == tm)
    def _():
        o_ref[...] = compute(0, tm)

    @pl.when(rows < tm)
    def _():
        for s in range(tm // sub):
            @pl.when(s * sub < rows)
            def _():
                o_ref[s * sub:(s + 1) * sub, :] = compute(s * sub, sub)

            @pl.when(s * sub >= rows)
            def _():
                o_ref[s * sub:(s + 1) * sub, :] = jnp.zeros((sub, o_ref.shape[1]), o_ref.dtype)


def _gateup_kernel(te_ref, na_ref, tend_ref, tr_ref, x_ref, wg_hbm, wu_hbm, o_ref,
                   sg_ref, su_ref, wgb_ref, wub_ref, sem, *, li, tn, sub):
    def copies(e, j):
        cols = pl.ds(pl.multiple_of(j * tn, tn), tn)
        return (pltpu.make_async_copy(wg_hbm.at[li, e, :, cols], sg_ref, sem.at[0]),
                pltpu.make_async_copy(wu_hbm.at[li, e, :, cols], su_ref, sem.at[1]))

    def cast():
        wgb_ref[...] = sg_ref[...].astype(BF16)
        wub_ref[...] = su_ref[...].astype(BF16)

    _weight_stream(te_ref, na_ref, tend_ref, copies, cast)

    def compute(r0, n):
        x = x_ref[r0:r0 + n, :]
        gate = jnp.dot(x, wgb_ref[...], preferred_element_type=F32)
        up = jnp.dot(x, wub_ref[...], preferred_element_type=F32)
        return (_silu(gate) * up).astype(o_ref.dtype)

    _row_tile_compute(tr_ref[pl.program_id(1)], x_ref.shape[0], sub, compute, o_ref)


def _down_kernel(te_ref, na_ref, tend_ref, tr_ref, h_ref, w_hbm, o_ref, s_ref, wb_ref, sem, *, li, tn, sub):
    def copies(e, j):
        cols = pl.ds(pl.multiple_of(j * tn, tn), tn)
        return (pltpu.make_async_copy(w_hbm.at[li, e, :, cols], s_ref, sem.at[0]),)

    def cast():
        wb_ref[...] = s_ref[...].astype(BF16)

    _weight_stream(te_ref, na_ref, tend_ref, copies, cast)

    def compute(r0, n):
        return jnp.dot(h_ref[r0:r0 + n, :], wb_ref[...], preferred_element_type=F32)

    _row_tile_compute(tr_ref[pl.program_id(1)], h_ref.shape[0], sub, compute, o_ref)


def grouped_swiglu(xs, wg, wu, wd, li, tile_expert, n_active, tile_end, tile_rows, tm):
    S, D = xs.shape
    Fd = wg.shape[-1]
    nT = S // tm
    tn = _tile(Fd, 512, V7X_LANES)
    sub = _tile(tm, 128, V7X_BF16_SUBLANES)

    def row_map(j, i, te, na, tend, tr):
        return (jnp.minimum(i, na[0] - 1), 0)

    def out_map(j, i, te, na, tend, tr):
        return (i, j)

    h = pl.pallas_call(
        functools.partial(_gateup_kernel, li=li, tn=tn, sub=sub),
        grid_spec=pltpu.PrefetchScalarGridSpec(
            num_scalar_prefetch=4, grid=(Fd // tn, nT),
            in_specs=[pl.BlockSpec((tm, D), row_map), ANY, ANY],
            out_specs=pl.BlockSpec((tm, tn), out_map),
            scratch_shapes=[pltpu.VMEM((D, tn), F32), pltpu.VMEM((D, tn), F32),
                            pltpu.VMEM((D, tn), BF16), pltpu.VMEM((D, tn), BF16),
                            pltpu.SemaphoreType.DMA((2,))]),
        out_shape=jax.ShapeDtypeStruct((S, Fd), BF16),
        compiler_params=_cparams(2), name="ffn_gate_up")(tile_expert, n_active, tile_end, tile_rows, xs, wg, wu)

    tn2 = _tile(D, 512, V7X_LANES)
    return pl.pallas_call(
        functools.partial(_down_kernel, li=li, tn=tn2, sub=sub),
        grid_spec=pltpu.PrefetchScalarGridSpec(
            num_scalar_prefetch=4, grid=(D // tn2, nT),
            in_specs=[pl.BlockSpec((tm, Fd), row_map), ANY],
            out_specs=pl.BlockSpec((tm, tn2), out_map),
            scratch_shapes=[pltpu.VMEM((Fd, tn2), F32), pltpu.VMEM((Fd, tn2), BF16),
                            pltpu.SemaphoreType.DMA((1,))]),
        out_shape=jax.ShapeDtypeStruct((S, D), F32),
        compiler_params=_cparams(2), name="ffn_down")(tile_expert, n_active, tile_end, tile_rows, h, wd)


def _router_kernel(x_ref, w_ref, b_ref, idx_ref, gate_ref):
    x = x_ref[...]
    w = w_ref[...]
    xh = x.astype(BF16)
    xl = (x - xh.astype(F32)).astype(BF16)
    wh = w.astype(BF16)
    wl = (w - wh.astype(F32)).astype(BF16)
    logits = (jnp.dot(xh, wh, preferred_element_type=F32) + jnp.dot(xh, wl, preferred_element_type=F32)
              + jnp.dot(xl, wh, preferred_element_type=F32)) + b_ref[...]
    col = lax.broadcasted_iota(jnp.int32, logits.shape, 1)
    big = jnp.int32(logits.shape[1])
    m1 = jnp.max(logits, axis=-1, keepdims=True)
    i1 = jnp.min(jnp.where(logits == m1, col, big), axis=-1, keepdims=True)
    rest = jnp.where(col == i1, -jnp.inf, logits)
    m2 = jnp.max(rest, axis=-1, keepdims=True)
    i2 = jnp.min(jnp.where(rest == m2, col, big), axis=-1, keepdims=True)
    e = jnp.exp(m2 - m1)
    g1 = 1.0 / (1.0 + e)
    g2 = e / (1.0 + e)
    idx_ref[...] = jnp.where(col == 0, i1, jnp.where(col == 1, i2, 0))
    gate_ref[...] = jnp.where(col == 0, g1, jnp.where(col == 1, g2, 0.0))


def router(x, w_pad, b_pad):
    M, D = x.shape
    tm = _tile(M, 512)
    out = pl.BlockSpec((tm, V7X_LANES), lambda i: (i, 0))
    return pl.pallas_call(
        _router_kernel, grid=(M // tm,),
        in_specs=[pl.BlockSpec((tm, D), lambda i: (i, 0)),
                  pl.BlockSpec((D, V7X_LANES), lambda i: (0, 0)),
                  pl.BlockSpec((1, V7X_LANES), lambda i: (0, 0))],
        out_specs=[out, out],
        out_shape=[jax.ShapeDtypeStruct((M, V7X_LANES), jnp.int32),
                   jax.ShapeDtypeStruct((M, V7X_LANES), F32)],
        compiler_params=_cparams(1), name="router")(x, w_pad, b_pad)


ISSUE_UNROLL = 8


def _dispatch_kernel(src_ref, na_ref, x_hbm, o_ref, buf_ref, sem, *, R):
    i = pl.program_id(0)
    na = na_ref[0]

    def issue_tile(t, slot):
        def body(r, carry):
            tok = src_ref[t * R + r]
            pltpu.make_async_copy(x_hbm.at[pl.ds(tok, 1), :], buf_ref.at[slot, pl.ds(r, 1), :],
                                  sem.at[slot]).start()
            return carry
        lax.fori_loop(0, R, body, 0, unroll=ISSUE_UNROLL)

    @pl.when(i == 0)
    def _():
        issue_tile(0, 0)

    @pl.when(i + 1 < na)
    def _():
        issue_tile(i + 1, (i + 1) % 2)

    @pl.when(i < na)
    def _():
        slot = i % 2
        pltpu.make_async_copy(x_hbm.at[pl.ds(0, R), :], buf_ref.at[slot], sem.at[slot]).wait()
        o_ref[...] = buf_ref[slot].astype(o_ref.dtype)

    @pl.when(i >= na)
    def _():
        o_ref[...] = jnp.zeros(o_ref.shape, o_ref.dtype)


def dispatch(x, src, n_active, S, R):
    M, D = x.shape
    kern = functools.partial(_dispatch_kernel, R=R)
    return pl.pallas_call(
        kern,
        grid_spec=pltpu.PrefetchScalarGridSpec(
            num_scalar_prefetch=2, grid=(S // R,),
            in_specs=[ANY],
            out_specs=pl.BlockSpec((R, D), lambda i, src, na: (i, 0)),
            scratch_shapes=[pltpu.VMEM((2, R, D), F32), pltpu.SemaphoreType.DMA((2,))]),
        out_shape=jax.ShapeDtypeStruct((S, D), BF16),
        compiler_params=_cparams(1), name="moe_dispatch")(src, n_active, x)


def _combine_ln_kernel(pos_ref, ys_hbm, x_ref, gate_ref, g_ref, b_ref, o0_ref, o1_ref, buf_ref, sem,
                       *, R, alpha, split_tile):
    i = pl.program_id(0)

    def issue_tile(t, slot):
        def body(r, carry):
            a = (t * R + r) * TOP_K
            for k in range(TOP_K):
                pltpu.make_async_copy(ys_hbm.at[pl.ds(pos_ref[a + k], 1), :],
                                      buf_ref.at[slot, k, pl.ds(r, 1), :], sem.at[slot]).start()
            return carry
        lax.fori_loop(0, R, body, 0, unroll=ISSUE_UNROLL // TOP_K)

    @pl.when(i == 0)
    def _():
        issue_tile(0, 0)

    @pl.when(i + 1 < pl.num_programs(0))
    def _():
        issue_tile(i + 1, (i + 1) % 2)

    slot = i % 2
    for k in range(TOP_K):
        pltpu.make_async_copy(ys_hbm.at[pl.ds(0, R), :], buf_ref.at[slot, k], sem.at[slot]).wait()
    gates = gate_ref[...]
    y = gates[:, 0:1] * buf_ref[slot, 0]
    for k in range(1, TOP_K):
        y = y + gates[:, k:k + 1] * buf_ref[slot, k]
    z = _layer_norm(alpha * x_ref[...] + y, g_ref[...], b_ref[...])
    _ln_emit(z, o0_ref, o1_ref, split_tile)


def combine_ln(ys, pos, x, gates, g, b, alpha, n_first=None):
    M, D = x.shape
    R = _ln_row_tile(M, n_first, 256)
    out_specs, out_shape, st = _ln_outputs(M, D, R, n_first, lambda fn: (lambda i, pos: fn(i)))
    kern = functools.partial(_combine_ln_kernel, R=R, alpha=alpha, split_tile=st)
    row = lambda i, pos: (i, 0)
    fix = lambda i, pos: (0, 0)
    return pl.pallas_call(
        kern,
        grid_spec=pltpu.PrefetchScalarGridSpec(
            num_scalar_prefetch=1, grid=(M // R,),
            in_specs=[ANY,
                      pl.BlockSpec((R, D), row),
                      pl.BlockSpec((R, V7X_LANES), row),
                      pl.BlockSpec((1, D), fix),
                      pl.BlockSpec((1, D), fix)],
            out_specs=out_specs,
            scratch_shapes=[pltpu.VMEM((2, TOP_K, R, D), F32), pltpu.SemaphoreType.DMA((2,))]),
        out_shape=out_shape,
        compiler_params=_cparams(1), name="moe_combine_ln")(pos, ys, x, gates, g, b)


def _routing_plan(idx, E, tm, n_tiles):
    e_flat = idx.reshape(-1)
    A = e_flat.shape[0]
    a_ids = jnp.arange(A, dtype=jnp.int32)
    order = jnp.sort(e_flat * A + a_ids) % A
    rank = jnp.argsort(order).astype(jnp.int32)
    counts = jnp.sum((e_flat[:, None] == jnp.arange(E, dtype=jnp.int32)[None, :]).astype(jnp.int32), axis=0)
    tiles_e = (counts + tm - 1) // tm
    tile_end = jnp.cumsum(tiles_e).astype(jnp.int32)
    tile_start = tile_end - tiles_e
    pad_start = tile_start * tm
    start = jnp.cumsum(counts) - counts
    n_active = tile_end[-1]
    pos = pad_start[e_flat] + rank - start[e_flat]
    tile_id = jnp.arange(n_tiles, dtype=jnp.int32)
    te = jnp.sum((jnp.minimum(tile_id, n_active - 1)[:, None] >= tile_end[None, :]).astype(jnp.int32), axis=1)
    tile_rows = jnp.where(tile_id < n_active,
                          jnp.clip(counts[te] - (tile_id - tile_start[te]) * tm, 0, tm), 0)
    slot = jnp.arange(n_tiles * tm, dtype=jnp.int32)
    se = te[slot // tm]
    off = slot - pad_start[se]
    r = start[se] + jnp.clip(off, 0, jnp.maximum(counts[se] - 1, 0))
    src = order[jnp.clip(r, 0, A - 1)] // TOP_K
    return (src.astype(jnp.int32), pos.astype(jnp.int32), te,
            n_active.reshape(1).astype(jnp.int32), tile_end, tile_rows.astype(jnp.int32))


def kernel(x_prompt, x_sample, cache_conv, state_ret, w_in, conv_w, conv_b, conv_ln_g, conv_ln_b,
           ret_gn_g, ret_gn_b, w_out, ln1_g, ln1_b, ln2_g, ln2_b, w_ff_gate, w_ff_up, w_ff_down,
           w_router, b_router, w_exp_gate, w_exp_up, w_exp_down):
    B, T, D = x_prompt.shape
    Bs, Ts, _ = x_sample.shape
    L = w_in.shape[0]
    K = conv_w.shape[1]
    C = conv_w.shape[2]
    H, DK = state_ret.shape[2], state_ret.shape[3]
    E = w_router.shape[-1]
    Np, Ns = B * T, Bs * Ts
    M = Np + Ns
    nC = C // V7X_LANES
    alpha = float((2 * L) ** 0.25)
    assert w_in.shape[-1] == 6 * C and H * DK == C and D == 2 * C

    x, xb = pack_tokens(x_prompt.reshape(Np, D), x_sample.reshape(Ns, D))

    tm_dense = _tile(M, 1024, V7X_BF16_SUBLANES)
    n_dense_tiles = M // tm_dense
    dense_te = jnp.zeros((n_dense_tiles,), jnp.int32)
    dense_na = jnp.full((1,), n_dense_tiles, jnp.int32)
    dense_rows = jnp.full((n_dense_tiles,), tm_dense, jnp.int32)
    w_ff_gate4, w_ff_up4, w_ff_down4 = w_ff_gate[:, None], w_ff_up[:, None], w_ff_down[:, None]
    moe_tm = _tile(M * TOP_K, MOE_TILE, V7X_BF16_SUBLANES)
    moe_tiles = (M * TOP_K) // moe_tm + E

    hc = jnp.zeros((M, C), BF16)
    hr = jnp.zeros((M, H * DK), BF16)
    conv_p = jnp.zeros((L, B, K - 1, C), F32)
    conv_s = jnp.zeros((L, Bs, K - 1, C), F32)
    ret_p = jnp.zeros((L, B, H, DK, DK), F32)
    ret_s = jnp.zeros((L, Bs, H, DK, DK), F32)
    for l in range(L):
        proj = matmul_ws(xb, w_in, l, BF16)

        w3 = conv_w[l].reshape(K, nC, V7X_LANES).transpose(1, 0, 2)
        cb3 = conv_b[l].reshape(nC, 1, V7X_LANES)
        cb, lg, lb = conv_b[l][None], conv_ln_g[l][None], conv_ln_b[l][None]
        hc, conv_p = conv_prompt(proj, w3, cb3, lg, lb, B, T, C, K, l, hc, conv_p)
        hc, conv_s = conv_sample(proj, cache_conv, l, conv_w[l], cb, lg, lb, Np, Bs, Ts, C, K, hc, conv_s)

        gng, gnb = ret_gn_g[l][None], ret_gn_b[l][None]
        hr, ret_p = retention(proj, gng, gnb, 0, B, T, H, DK, 0, l, hr, ret_p)
        hr, ret_s = retention(proj, gng, gnb, Np, Bs, Ts, H, DK, PAST_LEN, l, hr, ret_s, state=state_ret)

        x, xb = outproj_ln(hc, hr, x, cast_weight(w_out, l), ln1_g[l][None], ln1_b[l][None], alpha)
        n_first = Np if l == L - 1 else None

        if l % 2 == 0:
            f = grouped_swiglu(xb, w_ff_gate4, w_ff_up4, w_ff_down4, l // 2, dense_te, dense_na,
                               dense_na, dense_rows, tm_dense)
            x, xb = resid_ln(f, x, ln2_g[l][None], ln2_b[l][None], alpha, n_first)
        else:
            li = l // 2
            w_pad = jnp.zeros((D, V7X_LANES), F32).at[:, :E].set(w_router[li])
            b_pad = jnp.full((1, V7X_LANES), -1e30, F32).at[0, :E].set(b_router[li])
            idx_w, gate_w = router(x, w_pad, b_pad)
            src, pos, te, na, tend, trows = _routing_plan(idx_w[:, :TOP_K], E, moe_tm, moe_tiles)
            xs = dispatch(x, src, na, moe_tiles * moe_tm, moe_tm)
            ys = grouped_swiglu(xs, w_exp_gate, w_exp_up, w_exp_down, li, te, na, tend, trows, moe_tm)
            x, xb = combine_ln(ys, pos, x, gate_w, ln2_g[l][None], ln2_b[l][None], alpha, n_first)

    return (x.reshape(B, T, D), xb.reshape(Bs, Ts, D), conv_p, ret_p, conv_s, ret_s)
```

```python
import functools

import numpy as np
import jax
import jax.numpy as jnp
from jax import lax
from jax.experimental import pallas as pl
from jax.experimental.pallas import tpu as pltpu

F32 = jnp.float32
BF16 = jnp.bfloat16

LN_EPS = 1e-5
ROPE_BASE = 10000.0
RET_CHUNK = 128
PAST_LEN = 16384
TOP_K = 2

V7X_LANES = 128
V7X_BF16_SUBLANES = 16
V7X_VMEM_LIMIT_BYTES = 56 * 1024 * 1024

HIST_ROWS = 32
MOE_TILE = 512
RET_SEQS_PER_STEP = 4
RET_CHUNKS_PER_STEP = 4
ANY = pl.BlockSpec(memory_space=pl.ANY)


def _cparams(n_axes):
    return pltpu.CompilerParams(dimension_semantics=("arbitrary",) * n_axes,
                                vmem_limit_bytes=V7X_VMEM_LIMIT_BYTES)


def _tile(n, pref, mult=8):
    t = min(pref, n)
    while t > mult and (n % t or t % mult):
        t -= mult
    assert n % t == 0 and t % mult == 0, (n, pref, mult)
    return t


def _skip_refs(kern, start, count):
    def wrapped(*refs):
        return kern(*refs[:start], *refs[start + count:])
    return wrapped


def _layer_norm(y, g, b):
    mu = jnp.mean(y, axis=-1, keepdims=True)
    d = y - mu
    var = jnp.mean(d * d, axis=-1, keepdims=True)
    return d * lax.rsqrt(var + LN_EPS) * g + b


def _silu(z):
    return z * jax.nn.sigmoid(z)


def _cast_kernel(w_ref, o_ref):
    o_ref[...] = w_ref[...].astype(o_ref.dtype)


def cast_weight(w, l):
    _, K, N = w.shape
    tr = _tile(K, 512, V7X_BF16_SUBLANES)
    return pl.pallas_call(
        _cast_kernel, grid=(K // tr,),
        in_specs=[pl.BlockSpec((None, tr, N), lambda i: (l, i, 0))],
        out_specs=pl.BlockSpec((tr, N), lambda i: (i, 0)),
        out_shape=jax.ShapeDtypeStruct((K, N), BF16),
        compiler_params=_cparams(1), name="cast_weight")(w)


def _pack_tokens_kernel(xp_ref, xs_ref, x_ref, xb_ref, *, split_tile):
    @pl.when(pl.program_id(0) < split_tile)
    def _():
        x_ref[...] = xp_ref[...]
        xb_ref[...] = xp_ref[...].astype(BF16)

    @pl.when(pl.program_id(0) >= split_tile)
    def _():
        x_ref[...] = xs_ref[...]
        xb_ref[...] = xs_ref[...].astype(BF16)


def pack_tokens(xp, xs):
    (Np, D), Ns = xp.shape, xs.shape[0]
    tm = _tile(int(np.gcd(Np, Ns)), 512, V7X_BF16_SUBLANES)
    st = Np // tm
    out = pl.BlockSpec((tm, D), lambda i: (i, 0))
    return pl.pallas_call(
        functools.partial(_pack_tokens_kernel, split_tile=st), grid=((Np + Ns) // tm,),
        in_specs=[pl.BlockSpec((tm, D), lambda i: (jnp.minimum(i, st - 1), 0)),
                  pl.BlockSpec((tm, D), lambda i: (jnp.maximum(i - st, 0), 0))],
        out_specs=[out, out],
        out_shape=[jax.ShapeDtypeStruct((Np + Ns, D), F32), jax.ShapeDtypeStruct((Np + Ns, D), BF16)],
        compiler_params=_cparams(1), name="pack_tokens")(xp, xs)


def _matmul_ws_kernel(x_ref, w_ref, o_ref, wb_ref):
    @pl.when(pl.program_id(1) == 0)
    def _():
        wb_ref[...] = w_ref[...].astype(BF16)
    o_ref[...] = jnp.dot(x_ref[...], wb_ref[...], preferred_element_type=F32).astype(o_ref.dtype)


def matmul_ws(x, w, l, out_dtype):
    M, K = x.shape
    N = w.shape[-1]
    tm = _tile(M, 1024, V7X_BF16_SUBLANES)
    tn = _tile(N, 1024, V7X_LANES)
    return pl.pallas_call(
        _matmul_ws_kernel, grid=(N // tn, M // tm),
        in_specs=[pl.BlockSpec((tm, K), lambda j, i: (i, 0)),
                  pl.BlockSpec((None, K, tn), lambda j, i: (l, 0, j))],
        out_specs=pl.BlockSpec((tm, tn), lambda j, i: (i, j)),
        out_shape=jax.ShapeDtypeStruct((M, N), out_dtype),
        scratch_shapes=[pltpu.VMEM((K, tn), BF16)],
        compiler_params=_cparams(2), name="in_proj")(x, w)


def _conv_post(conv_chunks, lg_ref, lb_ref, write, C):
    s = conv_chunks[0]
    for ch in conv_chunks[1:]:
        s = s + ch
    mu = jnp.sum(s, axis=-1, keepdims=True) / C
    sq = None
    for ch in conv_chunks:
        d = ch - mu
        sq = d * d if sq is None else sq + d * d
    var = jnp.sum(sq, axis=-1, keepdims=True) / C
    inv = lax.rsqrt(var + LN_EPS)
    for c, ch in enumerate(conv_chunks):
        sl = slice(c * V7X_LANES, (c + 1) * V7X_LANES)
        z = (ch - mu) * inv * lg_ref[:, sl] + lb_ref[:, sl]
        write(c, _silu(z))


def _glu(a_ref, g_ref):
    return a_ref[...].astype(F32) * jax.nn.sigmoid(g_ref[...].astype(F32))


def _conv_prompt_kernel(a_ref, g_ref, ha_ref, hg_ref, w3_ref, cb3_ref, lg_ref, lb_ref,
                        o_ref, nb_ref, upad_ref, conv_ref, *, tt, K, C, rb):
    i = pl.program_id(1)
    nC = C // V7X_LANES
    off = HIST_ROWS - (K - 1)
    u = _glu(a_ref, g_ref)
    uh = jnp.where(i > 0, _glu(ha_ref, hg_ref), 0.0)
    for c in range(nC):
        sl = slice(c * V7X_LANES, (c + 1) * V7X_LANES)
        upad_ref[c, 0:HIST_ROWS, :] = uh[:, sl]
        upad_ref[c, HIST_ROWS:HIST_ROWS + tt, :] = u[:, sl]

    def chunk_body(c, carry):
        for r in range(tt // rb):
            acc = jnp.zeros((rb, V7X_LANES), F32)
            for j in range(K):
                lo = r * rb + off + j
                acc = acc + upad_ref[c, lo:lo + rb, :] * w3_ref[c, j:j + 1, :]
            conv_ref[c, r * rb:(r + 1) * rb, :] = acc + cb3_ref[c]
        return carry

    lax.fori_loop(0, nC, chunk_body, 0)
    def write(c, v):
        o_ref[:, c * V7X_LANES:(c + 1) * V7X_LANES] = v.astype(o_ref.dtype)

    _conv_post([conv_ref[c] for c in range(nC)], lg_ref, lb_ref, write, C)

    @pl.when(i == pl.num_programs(1) - 1)
    def _():
        for c in range(nC):
            sl = slice(c * V7X_LANES, (c + 1) * V7X_LANES)
            nb_ref[:, sl] = upad_ref[c, HIST_ROWS + tt - (K - 1):HIST_ROWS + tt, :]


def conv_prompt(proj, w3, cb3, lg, lb, B, T, C, K, l, hc, nb_prev):
    M = proj.shape[0]
    L = nb_prev.shape[0]
    tt = _tile(T, 256, HIST_ROWS)
    nT = T // tt
    nC = C // V7X_LANES
    hb = tt // HIST_ROWS
    kern = functools.partial(_conv_prompt_kernel, tt=tt, K=K, C=C, rb=_tile(tt, 64))
    hist = lambda col: (lambda b, i: (jnp.maximum((b * nT + i) * hb - 1, 0), col))
    in_specs = [pl.BlockSpec((tt, C), lambda b, i: (b * nT + i, 0)),
                pl.BlockSpec((tt, C), lambda b, i: (b * nT + i, 1)),
                pl.BlockSpec((HIST_ROWS, C), hist(0)),
                pl.BlockSpec((HIST_ROWS, C), hist(1)),
                pl.BlockSpec((nC, K, V7X_LANES), lambda b, i: (0, 0, 0)),
                pl.BlockSpec((nC, 1, V7X_LANES), lambda b, i: (0, 0, 0)),
                pl.BlockSpec((1, C), lambda b, i: (0, 0)),
                pl.BlockSpec((1, C), lambda b, i: (0, 0))]
    args = [proj, proj, proj, proj, w3, cb3, lg, lb, hc, nb_prev]
    in_specs += [ANY, ANY]
    aliases = {8: 0, 9: 1}
    return pl.pallas_call(
        _skip_refs(kern, 8, 2), grid=(B, nT), in_specs=in_specs,
        out_specs=[pl.BlockSpec((tt, C), lambda b, i: (b * nT + i, 0)),
                   pl.BlockSpec((None, None, K - 1, C), lambda b, i: (l, b, 0, 0))],
        out_shape=[jax.ShapeDtypeStruct((M, C), BF16),
                   jax.ShapeDtypeStruct((L, B, K - 1, C), F32)],
        scratch_shapes=[pltpu.VMEM((nC, HIST_ROWS + tt, V7X_LANES), F32),
                        pltpu.VMEM((nC, tt, V7X_LANES), F32)],
        input_output_aliases=aliases,
        compiler_params=_cparams(2), name="conv_prompt")(*args)


def _conv_sample_kernel(a_ref, g_ref, cache_ref, w_ref, cb_ref, lg_ref, lb_ref,
                        o_ref, nb_ref, u_ref, upad_ref, z_ref, *, T, K, C, bb):
    nC = C // V7X_LANES
    lanes = [slice(c * V7X_LANES, (c + 1) * V7X_LANES) for c in range(nC)]
    u = _glu(a_ref, g_ref)
    for c in range(nC):
        u_ref[c] = u[:, lanes[c]]
    upad_ref[0:K - 1] = cache_ref[...]
    for t in range(T):
        for c in range(nC):
            upad_ref[K - 1 + t, :, lanes[c]] = u_ref[c, pl.ds(t, bb, stride=T), :]
    nb_ref[...] = upad_ref[T:T + K - 1]
    chunks = []
    for c in range(nC):
        acc = jnp.zeros((T, bb, V7X_LANES), F32)
        for j in range(K):
            acc = acc + upad_ref[j:j + T, :, lanes[c]] * w_ref[j:j + 1, lanes[c]][None]
        chunks.append(acc.reshape(T * bb, V7X_LANES) + cb_ref[:, lanes[c]])

    def write(c, v):
        z_ref[c] = v

    _conv_post(chunks, lg_ref, lb_ref, write, C)
    for b in range(0, bb, 2):
        for c in range(nC):
            pair = jnp.concatenate([z_ref[c, pl.ds(b, T, stride=bb), :],
                                    z_ref[c, pl.ds(b + 1, T, stride=bb), :]], axis=0)
            o_ref[b * T:(b + 2) * T, lanes[c]] = pair.astype(o_ref.dtype)


def conv_sample(proj, cache_t, l, w, cb, lg, lb, row0, B, T, C, K, hc, nb_prev):
    assert T % 8 == 0
    L = cache_t.shape[0]
    bb = _tile(B, 16, 8)
    assert row0 % (bb * T) == 0
    blk0 = row0 // (bb * T)
    kern = functools.partial(_conv_sample_kernel, T=T, K=K, C=C, bb=bb)
    vec = pl.BlockSpec((1, C), lambda s: (0, 0))
    in_specs = [pl.BlockSpec((bb * T, C), lambda s: (blk0 + s, 0)),
                pl.BlockSpec((bb * T, C), lambda s: (blk0 + s, 1)),
                pl.BlockSpec((None, K - 1, bb, C), lambda s: (l, 0, s, 0)),
                pl.BlockSpec((K, C), lambda s: (0, 0)), vec, vec, vec, ANY, ANY]
    args = [proj, proj, cache_t, w, cb, lg, lb, hc, nb_prev]
    aliases = {7: 0, 8: 1}
    return pl.pallas_call(
        _skip_refs(kern, 7, 2), grid=(B // bb,), in_specs=in_specs,
        out_specs=[pl.BlockSpec((bb * T, C), lambda s: (blk0 + s, 0)),
                   pl.BlockSpec((None, K - 1, bb, C), lambda s: (l, 0, s, 0))],
        out_shape=[jax.ShapeDtypeStruct(hc.shape, BF16),
                   jax.ShapeDtypeStruct((L, K - 1, B, C), F32)],
        scratch_shapes=[pltpu.VMEM((C // V7X_LANES, bb * T, V7X_LANES), F32),
                        pltpu.VMEM((K - 1 + T, bb, C), F32),
                        pltpu.VMEM((C // V7X_LANES, T * bb, V7X_LANES), F32)],
        input_output_aliases=aliases,
        compiler_params=_cparams(1), name="conv_sample")(*args)


def _retention_tables(H, chunk, rows):
    gamma = (1.0 - 2.0 ** (-5.0 - np.arange(H, dtype=np.float32))).astype(np.float32)
    log_g = np.log(gamma).astype(np.float32)
    idx = np.arange(chunk, dtype=np.float32)
    diff = idx[:, None] - idx[None, :]
    dmask = np.where(diff[None] >= 0, np.exp(diff[None] * log_g[:, None, None]), 0.0).astype(np.float32)
    q_dec = np.exp((idx[None, :] + np.float32(1.0)) * log_g[:, None]).astype(np.float32)
    k_dec = np.exp((np.float32(chunk) - np.float32(1.0) - idx[None, :]) * log_g[:, None]).astype(np.float32)
    c_dec = np.exp(np.float32(chunk) * log_g).astype(np.float32)
    dm = np.zeros((H, rows, rows), np.float32)
    dm[:, :chunk, :chunk] = dmask
    qd = np.zeros((H, rows, 1), np.float32)
    qd[:, :chunk, 0] = q_dec
    kd = np.zeros((H, rows, 1), np.float32)
    kd[:, :chunk, 0] = k_dec
    return dm, qd, kd, [float(c) for c in c_dec]


def _rotary_tables(pos, half):
    inv = (1.0 / (ROPE_BASE ** (np.arange(half, dtype=np.float32) / half))).astype(np.float32)
    ang = (pos.astype(np.float32)[:, None] * inv[None, :]).astype(np.float32)
    return np.cos(ang).astype(np.float32), np.sin(ang).astype(np.float32)


def _retention_kernel(*refs, H, DK, C, nb, cps, rows, c_dec, has_s0):
    if has_s0:
        (q_ref, k_ref, v_ref, g_ref, cos_ref, sin_ref, dm_ref, qd_ref, kd_ref, gng_ref, gnb_ref,
         s0_ref, o_ref, ns_ref) = refs
    else:
        (q_ref, k_ref, v_ref, g_ref, cos_ref, sin_ref, dm_ref, qd_ref, kd_ref, gng_ref, gnb_ref,
         o_ref, ns_ref) = refs
        s0_ref = None
    half = DK // 2
    scale = DK ** -0.5

    @pl.when(pl.program_id(1) == 0)
    def _():
        if has_s0:
            ns_ref[...] = s0_ref[...]
        else:
            ns_ref[...] = jnp.zeros(ns_ref.shape, F32)

    q_all = q_ref[...].astype(F32)
    k_all = k_ref[...].astype(F32)
    v_all = v_ref[...].astype(F32)
    g_all = g_ref[...].astype(F32)

    def rot(x, cos, sin):
        x1, x2 = x[:, :half], x[:, half:]
        return jnp.concatenate([x1 * cos - x2 * sin, x1 * sin + x2 * cos], axis=-1)

    def pad_rows(x):
        if rows == C:
            return x
        return jnp.concatenate([x, jnp.zeros((rows - C, x.shape[1]), x.dtype)], axis=0)

    outs = []
    for n in range(nb * cps):
        s, cc = divmod(n, cps)
        rs = slice(n * C, (n + 1) * C)
        cos = cos_ref[cc * C:(cc + 1) * C, :]
        sin = sin_ref[cc * C:(cc + 1) * C, :]
        heads = []
        for h in range(H):
            hs = slice(h * DK, (h + 1) * DK)
            qr = pad_rows(rot(q_all[rs, hs], cos, sin))
            kr = pad_rows(rot(k_all[rs, hs], cos, sin) * scale)
            vb = pad_rows(v_all[rs, hs]).astype(BF16)
            S = ns_ref[s, h]
            scores = lax.dot_general(qr.astype(BF16), kr.astype(BF16), (((1,), (1,)), ((), ())),
                                     preferred_element_type=F32) * dm_ref[h]
            o = (jnp.dot(scores.astype(BF16), vb, preferred_element_type=F32)
                 + jnp.dot((qr * qd_ref[h]).astype(BF16), S.astype(BF16), preferred_element_type=F32))
            kd = (kr * kd_ref[h]).astype(BF16)
            ns_ref[s, h] = c_dec[h] * S + lax.dot_general(kd, vb, (((0,), (0,)), ((), ())),
                                                          preferred_element_type=F32)
            o = o[:C]
            mu = jnp.mean(o, axis=-1, keepdims=True)
            d = o - mu
            var = jnp.mean(d * d, axis=-1, keepdims=True)
            on = d * lax.rsqrt(var + LN_EPS) * gng_ref[:, hs] + gnb_ref[:, hs]
            heads.append(_silu(g_all[rs, hs]) * on)
        outs.append(jnp.concatenate(heads, axis=-1))
    o_ref[...] = jnp.concatenate(outs, axis=0).astype(o_ref.dtype)


def retention(proj, gng, gnb, row0, B, T, H, DK, pos0, l, hr, ns_prev, state=None):
    M = proj.shape[0]
    L = ns_prev.shape[0]
    R = H * DK
    C = int(np.gcd(T, RET_CHUNK))
    n_chunks = T // C
    nb = 1 if n_chunks > 1 else _tile(B, RET_SEQS_PER_STEP, 1)
    cps = _tile(n_chunks, RET_CHUNKS_PER_STEP, 1)
    n_steps = n_chunks // cps
    rows = max(C, V7X_BF16_SUBLANES)
    dm, qd, kd, c_dec = _retention_tables(H, C, rows)
    cos, sin = _rotary_tables(pos0 + np.arange(T), DK // 2)
    blk_rows = nb * cps * C
    assert row0 % blk_rows == 0 and (blk_rows % V7X_BF16_SUBLANES == 0)
    blk0 = row0 // blk_rows
    has_s0 = state is not None
    kern = functools.partial(_retention_kernel, H=H, DK=DK, C=C, nb=nb, cps=cps, rows=rows,
                             c_dec=c_dec, has_s0=has_s0)

    def row_map(col):
        return lambda sb, c: (blk0 + sb * n_steps + c, col)

    in_specs = [pl.BlockSpec((blk_rows, R), row_map(2)),
                pl.BlockSpec((blk_rows, R), row_map(3)),
                pl.BlockSpec((blk_rows, R), row_map(4)),
                pl.BlockSpec((blk_rows, R), row_map(5)),
                pl.BlockSpec((cps * C, DK // 2), lambda sb, c: (c, 0)),
                pl.BlockSpec((cps * C, DK // 2), lambda sb, c: (c, 0)),
                pl.BlockSpec((H, rows, rows), lambda sb, c: (0, 0, 0)),
                pl.BlockSpec((H, rows, 1), lambda sb, c: (0, 0, 0)),
                pl.BlockSpec((H, rows, 1), lambda sb, c: (0, 0, 0)),
                pl.BlockSpec((1, R), lambda sb, c: (0, 0)),
                pl.BlockSpec((1, R), lambda sb, c: (0, 0))]
    args = [proj, proj, proj, proj, jnp.asarray(cos), jnp.asarray(sin), jnp.asarray(dm),
            jnp.asarray(qd), jnp.asarray(kd), gng, gnb]
    if has_s0:
        in_specs.append(pl.BlockSpec((None, nb, H, DK, DK), lambda sb, c: (l, sb, 0, 0, 0)))
        args.append(state)
    n_used = len(args)
    aliases = {n_used: 0, n_used + 1: 1}
    in_specs += [ANY, ANY]
    args += [hr, ns_prev]
    return pl.pallas_call(
        _skip_refs(kern, n_used, 2), grid=(B // nb, n_steps),
        in_specs=in_specs,
        out_specs=[pl.BlockSpec((blk_rows, R), lambda sb, c: (blk0 + sb * n_steps + c, 0)),
                   pl.BlockSpec((None, nb, H, DK, DK), lambda sb, c: (l, sb, 0, 0, 0))],
        out_shape=[jax.ShapeDtypeStruct((M, R), BF16),
                   jax.ShapeDtypeStruct((L, B, H, DK, DK), F32)],
        input_output_aliases=aliases,
        compiler_params=_cparams(2), name="retention")(*args)


def _outproj_ln_kernel(hc_ref, hr_ref, x_ref, w_ref, g_ref, b_ref, xo_ref, xb_ref, *, alpha, cc, sub):
    for s in range(x_ref.shape[0] // sub):
        rs = slice(s * sub, (s + 1) * sub)
        m = (jnp.dot(hc_ref[rs, :], w_ref[0:cc, :], preferred_element_type=F32)
             + jnp.dot(hr_ref[rs, :], w_ref[cc:, :], preferred_element_type=F32))
        z = _layer_norm(alpha * x_ref[rs, :] + m, g_ref[...], b_ref[...])
        xo_ref[rs, :] = z
        xb_ref[rs, :] = z.astype(BF16)


def outproj_ln(hc, hr, x, wb, g, b, alpha):
    M, D = x.shape
    cc = hc.shape[1]
    tm = _tile(M, 512, V7X_BF16_SUBLANES)
    kern = functools.partial(_outproj_ln_kernel, alpha=alpha, cc=cc, sub=_tile(tm, 128, V7X_BF16_SUBLANES))
    return pl.pallas_call(
        kern, grid=(M // tm,),
        in_specs=[pl.BlockSpec((tm, cc), lambda i: (i, 0)),
                  pl.BlockSpec((tm, hr.shape[1]), lambda i: (i, 0)),
                  pl.BlockSpec((tm, D), lambda i: (i, 0)),
                  pl.BlockSpec(wb.shape, lambda i: (0, 0)),
                  pl.BlockSpec((1, D), lambda i: (0, 0)),
                  pl.BlockSpec((1, D), lambda i: (0, 0))],
        out_specs=[pl.BlockSpec((tm, D), lambda i: (i, 0)),
                   pl.BlockSpec((tm, D), lambda i: (i, 0))],
        out_shape=[jax.ShapeDtypeStruct((M, D), F32), jax.ShapeDtypeStruct((M, D), BF16)],
        compiler_params=_cparams(1), name="outproj_ln")(hc, hr, x, wb, g, b)


def _ln_row_tile(M, n_first, pref):
    n = M if n_first is None else int(np.gcd(n_first, M - n_first))
    return _tile(n, pref, V7X_BF16_SUBLANES)


def _ln_outputs(M, D, tm, n_first, wrap):
    if n_first is None:
        row = pl.BlockSpec((tm, D), wrap(lambda i: (i, 0)))
        return [row, row], [jax.ShapeDtypeStruct((M, D), F32), jax.ShapeDtypeStruct((M, D), BF16)], None
    st = n_first // tm
    specs = [pl.BlockSpec((tm, D), wrap(lambda i: (jnp.minimum(i, st - 1), 0))),
             pl.BlockSpec((tm, D), wrap(lambda i: (jnp.maximum(i - st, 0), 0)))]
    return specs, [jax.ShapeDtypeStruct((n_first, D), F32), jax.ShapeDtypeStruct((M - n_first, D), F32)], st


def _ln_emit(z, o0_ref, o1_ref, split_tile):
    if split_tile is None:
        o0_ref[...] = z
        o1_ref[...] = z.astype(BF16)
    else:
        @pl.when(pl.program_id(0) < split_tile)
        def _():
            o0_ref[...] = z

        @pl.when(pl.program_id(0) >= split_tile)
        def _():
            o1_ref[...] = z


def _resid_ln_kernel(f_ref, x_ref, g_ref, b_ref, o0_ref, o1_ref, *, alpha, split_tile):
    z = _layer_norm(alpha * x_ref[...] + f_ref[...], g_ref[...], b_ref[...])
    _ln_emit(z, o0_ref, o1_ref, split_tile)


def resid_ln(f, x, g, b, alpha, n_first=None):
    M, D = x.shape
    tm = _ln_row_tile(M, n_first, 512)
    out_specs, out_shape, st = _ln_outputs(M, D, tm, n_first, lambda fn: fn)
    kern = functools.partial(_resid_ln_kernel, alpha=alpha, split_tile=st)
    row = pl.BlockSpec((tm, D), lambda i: (i, 0))
    vec = pl.BlockSpec((1, D), lambda i: (0, 0))
    return pl.pallas_call(
        kern, grid=(M // tm,), in_specs=[row, row, vec, vec], out_specs=out_specs, out_shape=out_shape,
        compiler_params=_cparams(1), name="resid_ln")(f, x, g, b)


def _tile_state(te_ref, na_ref):
    i = pl.program_id(1)
    active = i < na_ref[0]
    changed = jnp.logical_or(i == 0, te_ref[i] != te_ref[jnp.maximum(i - 1, 0)])
    return active, changed


def _weight_stream(te_ref, na_ref, tend_ref, copies, cast):
    j = pl.program_id(0)
    i = pl.program_id(1)
    active, changed = _tile_state(te_ref, na_ref)

    @pl.when(jnp.logical_and(active, changed))
    def _():
        e = te_ref[i]

        @pl.when(jnp.logical_and(j == 0, i == 0))
        def _():
            for c in copies(e, j):
                c.start()
        for c in copies(e, j):
            c.wait()
        cast()
        nxt = tend_ref[e]
        same_j = nxt < na_ref[0]
        e2 = te_ref[jnp.where(same_j, nxt, 0)]
        j2 = jnp.where(same_j, j, j + 1)

        @pl.when(j2 < pl.num_programs(0))
        def _():
            for c in copies(e2, j2):
                c.start()
    return active


def _row_tile_compute(rows, tm, sub, compute, o_ref):
    @pl.when(rows == tm)
    def _():
        o_ref[...] = compute(0, tm)

    @pl.when(rows < tm)
    def _():
        for s in range(tm // sub):
            @pl.when(s * sub < rows)
            def _():
                o_ref[s * sub:(s + 1) * sub, :] = compute(s * sub, sub)

            @pl.when(s * sub >= rows)
            def _():
                o_ref[s * sub:(s + 1) * sub, :] = jnp.zeros((sub, o_ref.shape[1]), o_ref.dtype)


def _gateup_kernel(te_ref, na_ref, tend_ref, tr_ref, x_ref, wg_hbm, wu_hbm, o_ref,
                   sg_ref, su_ref, wgb_ref, wub_ref, sem, *, li, tn, sub):
    def copies(e, j):
        cols = pl.ds(pl.multiple_of(j * tn, tn), tn)
        return (pltpu.make_async_copy(wg_hbm.at[li, e, :, cols], sg_ref, sem.at[0]),
                pltpu.make_async_copy(wu_hbm.at[li, e, :, cols], su_ref, sem.at[1]))

    def cast():
        wgb_ref[...] = sg_ref[...].astype(BF16)
        wub_ref[...] = su_ref[...].astype(BF16)

    _weight_stream(te_ref, na_ref, tend_ref, copies, cast)

    def compute(r0, n):
        x = x_ref[r0:r0 + n, :]
        gate = jnp.dot(x, wgb_ref[...], preferred_element_type=F32)
        up = jnp.dot(x, wub_ref[...], preferred_element_type=F32)
        return (_silu(gate) * up).astype(o_ref.dtype)

    _row_tile_compute(tr_ref[pl.program_id(1)], x_ref.shape[0], sub, compute, o_ref)


def _down_kernel(te_ref, na_ref, tend_ref, tr_ref, h_ref, w_hbm, o_ref, s_ref, wb_ref, sem, *, li, tn, sub):
    def copies(e, j):
        cols = pl.ds(pl.multiple_of(j * tn, tn), tn)
        return (pltpu.make_async_copy(w_hbm.at[li, e, :, cols], s_ref, sem.at[0]),)

    def cast():
        wb_ref[...] = s_ref[...].astype(BF16)

    _weight_stream(te_ref, na_ref, tend_ref, copies, cast)

    def compute(r0, n):
        return jnp.dot(h_ref[r0:r0 + n, :], wb_ref[...], preferred_element_type=F32)

    _row_tile_compute(tr_ref[pl.program_id(1)], h_ref.shape[0], sub, compute, o_ref)


def grouped_swiglu(xs, wg, wu, wd, li, tile_expert, n_active, tile_end, tile_rows, tm):
    S, D = xs.shape
    Fd = wg.shape[-1]
    nT = S // tm
    tn = _tile(Fd, 512, V7X_LANES)
    sub = _tile(tm, 128, V7X_BF16_SUBLANES)

    def row_map(j, i, te, na, tend, tr):
        return (jnp.minimum(i, na[0] - 1), 0)

    def out_map(j, i, te, na, tend, tr):
        return (i, j)

    h = pl.pallas_call(
        functools.partial(_gateup_kernel, li=li, tn=tn, sub=sub),
        grid_spec=pltpu.PrefetchScalarGridSpec(
            num_scalar_prefetch=4, grid=(Fd // tn, nT),
            in_specs=[pl.BlockSpec((tm, D), row_map), ANY, ANY],
            out_specs=pl.BlockSpec((tm, tn), out_map),
            scratch_shapes=[pltpu.VMEM((D, tn), F32), pltpu.VMEM((D, tn), F32),
                            pltpu.VMEM((D, tn), BF16), pltpu.VMEM((D, tn), BF16),
                            pltpu.SemaphoreType.DMA((2,))]),
        out_shape=jax.ShapeDtypeStruct((S, Fd), BF16),
        compiler_params=_cparams(2), name="ffn_gate_up")(tile_expert, n_active, tile_end, tile_rows, xs, wg, wu)

    tn2 = _tile(D, 512, V7X_LANES)
    return pl.pallas_call(
        functools.partial(_down_kernel, li=li, tn=tn2, sub=sub),
        grid_spec=pltpu.PrefetchScalarGridSpec(
            num_scalar_prefetch=4, grid=(D // tn2, nT),
            in_specs=[pl.BlockSpec((tm, Fd), row_map), ANY],
            out_specs=pl.BlockSpec((tm, tn2), out_map),
            scratch_shapes=[pltpu.VMEM((Fd, tn2), F32), pltpu.VMEM((Fd, tn2), BF16),
                            pltpu.SemaphoreType.DMA((1,))]),
        out_shape=jax.ShapeDtypeStruct((S, D), F32),
        compiler_params=_cparams(2), name="ffn_down")(tile_expert, n_active, tile_end, tile_rows, h, wd)


def _router_kernel(x_ref, w_ref, b_ref, idx_ref, gate_ref):
    x = x_ref[...]
    w = w_ref[...]
    xh = x.astype(BF16)
    xl = (x - xh.astype(F32)).astype(BF16)
    wh = w.astype(BF16)
    wl = (w - wh.astype(F32)).astype(BF16)
    logits = (jnp.dot(xh, wh, preferred_element_type=F32) + jnp.dot(xh, wl, preferred_element_type=F32)
              + jnp.dot(xl, wh, preferred_element_type=F32)) + b_ref[...]
    col = lax.broadcasted_iota(jnp.int32, logits.shape, 1)
    big = jnp.int32(logits.shape[1])
    m1 = jnp.max(logits, axis=-1, keepdims=True)
    i1 = jnp.min(jnp.where(logits == m1, col, big), axis=-1, keepdims=True)
    rest = jnp.where(col == i1, -jnp.inf, logits)
    m2 = jnp.max(rest, axis=-1, keepdims=True)
    i2 = jnp.min(jnp.where(rest == m2, col, big), axis=-1, keepdims=True)
    e = jnp.exp(m2 - m1)
    g1 = 1.0 / (1.0 + e)
    g2 = e / (1.0 + e)
    idx_ref[...] = jnp.where(col == 0, i1, jnp.where(col == 1, i2, 0))
    gate_ref[...] = jnp.where(col == 0, g1, jnp.where(col == 1, g2, 0.0))


def router(x, w_pad, b_pad):
    M, D = x.shape
    tm = _tile(M, 512)
    out = pl.BlockSpec((tm, V7X_LANES), lambda i: (i, 0))
    return pl.pallas_call(
        _router_kernel, grid=(M // tm,),
        in_specs=[pl.BlockSpec((tm, D), lambda i: (i, 0)),
                  pl.BlockSpec((D, V7X_LANES), lambda i: (0, 0)),
                  pl.BlockSpec((1, V7X_LANES), lambda i: (0, 0))],
        out_specs=[out, out],
        out_shape=[jax.ShapeDtypeStruct((M, V7X_LANES), jnp.int32),
                   jax.ShapeDtypeStruct((M, V7X_LANES), F32)],
        compiler_params=_cparams(1), name="router")(x, w_pad, b_pad)


ISSUE_UNROLL = 8


def _dispatch_kernel(src_ref, na_ref, x_hbm, o_ref, buf_ref, sem, *, R):
    i = pl.program_id(0)
    na = na_ref[0]

    def issue_tile(t, slot):
        def body(r, carry):
            tok = src_ref[t * R + r]
            pltpu.make_async_copy(x_hbm.at[pl.ds(tok, 1), :], buf_ref.at[slot, pl.ds(r, 1), :],
                                  sem.at[slot]).start()
            return carry
        lax.fori_loop(0, R, body, 0, unroll=ISSUE_UNROLL)

    @pl.when(i == 0)
    def _():
        issue_tile(0, 0)

    @pl.when(i + 1 < na)
    def _():
        issue_tile(i + 1, (i + 1) % 2)

    @pl.when(i < na)
    def _():
        slot = i % 2
        pltpu.make_async_copy(x_hbm.at[pl.ds(0, R), :], buf_ref.at[slot], sem.at[slot]).wait()
        o_ref[...] = buf_ref[slot].astype(o_ref.dtype)

    @pl.when(i >= na)
    def _():
        o_ref[...] = jnp.zeros(o_ref.shape, o_ref.dtype)


def dispatch(x, src, n_active, S, R):
    M, D = x.shape
    kern = functools.partial(_dispatch_kernel, R=R)
    return pl.pallas_call(
        kern,
        grid_spec=pltpu.PrefetchScalarGridSpec(
            num_scalar_prefetch=2, grid=(S // R,),
            in_specs=[ANY],
            out_specs=pl.BlockSpec((R, D), lambda i, src, na: (i, 0)),
            scratch_shapes=[pltpu.VMEM((2, R, D), F32), pltpu.SemaphoreType.DMA((2,))]),
        out_shape=jax.ShapeDtypeStruct((S, D), BF16),
        compiler_params=_cparams(1), name="moe_dispatch")(src, n_active, x)


def _combine_ln_kernel(pos_ref, ys_hbm, x_ref, gate_ref, g_ref, b_ref, o0_ref, o1_ref, buf_ref, sem,
                       *, R, alpha, split_tile):
    i = pl.program_id(0)

    def issue_tile(t, slot):
        def body(r, carry):
            a = (t * R + r) * TOP_K
            for k in range(TOP_K):
                pltpu.make_async_copy(ys_hbm.at[pl.ds(pos_ref[a + k], 1), :],
                                      buf_ref.at[slot, k, pl.ds(r, 1), :], sem.at[slot]).start()
            return carry
        lax.fori_loop(0, R, body, 0, unroll=ISSUE_UNROLL // TOP_K)

    @pl.when(i == 0)
    def _():
        issue_tile(0, 0)

    @pl.when(i + 1 < pl.num_programs(0))
    def _():
        issue_tile(i + 1, (i + 1) % 2)

    slot = i % 2
    for k in range(TOP_K):
        pltpu.make_async_copy(ys_hbm.at[pl.ds(0, R), :], buf_ref.at[slot, k], sem.at[slot]).wait()
    gates = gate_ref[...]
    y = gates[:, 0:1] * buf_ref[slot, 0]
    for k in range(1, TOP_K):
        y = y + gates[:, k:k + 1] * buf_ref[slot, k]
    z = _layer_norm(alpha * x_ref[...] + y, g_ref[...], b_ref[...])
    _ln_emit(z, o0_ref, o1_ref, split_tile)


def combine_ln(ys, pos, x, gates, g, b, alpha, n_first=None):
    M, D = x.shape
    R = _ln_row_tile(M, n_first, 256)
    out_specs, out_shape, st = _ln_outputs(M, D, R, n_first, lambda fn: (lambda i, pos: fn(i)))
    kern = functools.partial(_combine_ln_kernel, R=R, alpha=alpha, split_tile=st)
    row = lambda i, pos: (i, 0)
    fix = lambda i, pos: (0, 0)
    return pl.pallas_call(
        kern,
        grid_spec=pltpu.PrefetchScalarGridSpec(
            num_scalar_prefetch=1, grid=(M // R,),
            in_specs=[ANY,
                      pl.BlockSpec((R, D), row),
                      pl.BlockSpec((R, V7X_LANES), row),
                      pl.BlockSpec((1, D), fix),
                      pl.BlockSpec((1, D), fix)],
            out_specs=out_specs,
            scratch_shapes=[pltpu.VMEM((2, TOP_K, R, D), F32), pltpu.SemaphoreType.DMA((2,))]),
        out_shape=out_shape,
        compiler_params=_cparams(1), name="moe_combine_ln")(pos, ys, x, gates, g, b)


def _routing_plan(idx, E, tm, n_tiles):
    e_flat = idx.reshape(-1)
    A = e_flat.shape[0]
    a_ids = jnp.arange(A, dtype=jnp.int32)
    order = jnp.sort(e_flat * A + a_ids) % A
    rank = jnp.argsort(order).astype(jnp.int32)
    counts = jnp.sum((e_flat[:, None] == jnp.arange(E, dtype=jnp.int32)[None, :]).astype(jnp.int32), axis=0)
    tiles_e = (counts + tm - 1) // tm
    tile_end = jnp.cumsum(tiles_e).astype(jnp.int32)
    tile_start = tile_end - tiles_e
    pad_start = tile_start * tm
    start = jnp.cumsum(counts) - counts
    n_active = tile_end[-1]
    pos = pad_start[e_flat] + rank - start[e_flat]
    tile_id = jnp.arange(n_tiles, dtype=jnp.int32)
    te = jnp.sum((jnp.minimum(tile_id, n_active - 1)[:, None] >= tile_end[None, :]).astype(jnp.int32), axis=1)
    tile_rows = jnp.where(tile_id < n_active,
                          jnp.clip(counts[te] - (tile_id - tile_start[te]) * tm, 0, tm), 0)
    slot = jnp.arange(n_tiles * tm, dtype=jnp.int32)
    se = te[slot // tm]
    off = slot - pad_start[se]
    r = start[se] + jnp.clip(off, 0, jnp.maximum(counts[se] - 1, 0))
    src = order[jnp.clip(r, 0, A - 1)] // TOP_K
    return (src.astype(jnp.int32), pos.astype(jnp.int32), te,
            n_active.reshape(1).astype(jnp.int32), tile_end, tile_rows.astype(jnp.int32))


def kernel(x_prompt, x_sample, cache_conv, state_ret, w_in, conv_w, conv_b, conv_ln_g, conv_ln_b,
           ret_gn_g, ret_gn_b, w_out, ln1_g, ln1_b, ln2_g, ln2_b, w_ff_gate, w_ff_up, w_ff_down,
           w_router, b_router, w_exp_gate, w_exp_up, w_exp_down):
    B, T, D = x_prompt.shape
    Bs, Ts, _ = x_sample.shape
    L = w_in.shape[0]
    K = conv_w.shape[1]
    C = conv_w.shape[2]
    H, DK = state_ret.shape[2], state_ret.shape[3]
    E = w_router.shape[-1]
    Np, Ns = B * T, Bs * Ts
    M = Np + Ns
    nC = C // V7X_LANES
    alpha = float((2 * L) ** 0.25)
    assert w_in.shape[-1] == 6 * C and H * DK == C and D == 2 * C

    x, xb = pack_tokens(x_prompt.reshape(Np, D), x_sample.reshape(Ns, D))

    tm_dense = _tile(M, 1024, V7X_BF16_SUBLANES)
    n_dense_tiles = M // tm_dense
    dense_te = jnp.zeros((n_dense_tiles,), jnp.int32)
    dense_na = jnp.full((1,), n_dense_tiles, jnp.int32)
    dense_rows = jnp.full((n_dense_tiles,), tm_dense, jnp.int32)
    w_ff_gate4, w_ff_up4, w_ff_down4 = w_ff_gate[:, None], w_ff_up[:, None], w_ff_down[:, None]
    moe_tm = _tile(M * TOP_K, MOE_TILE, V7X_BF16_SUBLANES)
    moe_tiles = (M * TOP_K) // moe_tm + E

    hc = jnp.zeros((M, C), BF16)
    hr = jnp.zeros((M, H * DK), BF16)
    conv_p = jnp.zeros((L, B, K - 1, C), F32)
    conv_s = jnp.zeros((L, K - 1, Bs, C), F32)
    cache_t = cache_conv.transpose(0, 2, 1, 3)
    ret_p = jnp.zeros((L, B, H, DK, DK), F32)
    ret_s = jnp.zeros((L, Bs, H, DK, DK), F32)
    for l in range(L):
        proj = matmul_ws(xb, w_in, l, BF16)

        w3 = conv_w[l].reshape(K, nC, V7X_LANES).transpose(1, 0, 2)
        cb3 = conv_b[l].reshape(nC, 1, V7X_LANES)
        cb, lg, lb = conv_b[l][None], conv_ln_g[l][None], conv_ln_b[l][None]
        hc, conv_p = conv_prompt(proj, w3, cb3, lg, lb, B, T, C, K, l, hc, conv_p)
        hc, conv_s = conv_sample(proj, cache_t, l, conv_w[l], cb, lg, lb, Np, Bs, Ts, C, K, hc, conv_s)

        gng, gnb = ret_gn_g[l][None], ret_gn_b[l][None]
        hr, ret_p = retention(proj, gng, gnb, 0, B, T, H, DK, 0, l, hr, ret_p)
        hr, ret_s = retention(proj, gng, gnb, Np, Bs, Ts, H, DK, PAST_LEN, l, hr, ret_s, state=state_ret)

        x, xb = outproj_ln(hc, hr, x, cast_weight(w_out, l), ln1_g[l][None], ln1_b[l][None], alpha)
        n_first = Np if l == L - 1 else None

        if l % 2 == 0:
            f = grouped_swiglu(xb, w_ff_gate4, w_ff_up4, w_ff_down4, l // 2, dense_te, dense_na,
                               dense_na, dense_rows, tm_dense)
            x, xb = resid_ln(f, x, ln2_g[l][None], ln2_b[l][None], alpha, n_first)
        else:
            li = l // 2
            w_pad = jnp.zeros((D, V7X_LANES), F32).at[:, :E].set(w_router[li])
            b_pad = jnp.full((1, V7X_LANES), -1e30, F32).at[0, :E].set(b_router[li])
            idx_w, gate_w = router(x, w_pad, b_pad)
            src, pos, te, na, tend, trows = _routing_plan(idx_w[:, :TOP_K], E, moe_tm, moe_tiles)
            xs = dispatch(x, src, na, moe_tiles * moe_tm, moe_tm)
            ys = grouped_swiglu(xs, w_exp_gate, w_exp_up, w_exp_down, li, te, na, tend, trows, moe_tm)
            x, xb = combine_ln(ys, pos, x, gate_w, ln2_g[l][None], ln2_b[l][None], alpha, n_first)

    return (x.reshape(B, T, D), xb.reshape(Bs, Ts, D), conv_p, ret_p, conv_s.transpose(0, 2, 1, 3), ret_s)
```

```python
import functools

import numpy as np
import jax
import jax.numpy as jnp
from jax import lax
from jax.experimental import pallas as pl
from jax.experimental.pallas import tpu as pltpu

F32 = jnp.float32
BF16 = jnp.bfloat16

LN_EPS = 1e-5
ROPE_BASE = 10000.0
RET_CHUNK = 128
PAST_LEN = 16384
TOP_K = 2

V7X_LANES = 128
V7X_BF16_SUBLANES = 16
V7X_VMEM_LIMIT_BYTES = 56 * 1024 * 1024

HIST_ROWS = 32
MOE_TILE = 512
RET_SEQS_PER_STEP = 8
RET_CHUNKS_PER_STEP = 4
ANY = pl.BlockSpec(memory_space=pl.ANY)


def _cparams(n_axes):
    return pltpu.CompilerParams(dimension_semantics=("arbitrary",) * n_axes,
                                vmem_limit_bytes=V7X_VMEM_LIMIT_BYTES)


def _tile(n, pref, mult=8):
    t = min(pref, n)
    while t > mult and (n % t or t % mult):
        t -= mult
    assert n % t == 0 and t % mult == 0, (n, pref, mult)
    return t


def _skip_refs(kern, start, count):
    def wrapped(*refs):
        return kern(*refs[:start], *refs[start + count:])
    return wrapped


def _layer_norm(y, g, b):
    mu = jnp.mean(y, axis=-1, keepdims=True)
    d = y - mu
    var = jnp.mean(d * d, axis=-1, keepdims=True)
    return d * lax.rsqrt(var + LN_EPS) * g + b


def _silu(z):
    return z * jax.nn.sigmoid(z)


def _cast_kernel(w_ref, o_ref):
    o_ref[...] = w_ref[...].astype(o_ref.dtype)


def cast_weight(w, l):
    _, K, N = w.shape
    tr = _tile(K, 512, V7X_BF16_SUBLANES)
    return pl.pallas_call(
        _cast_kernel, grid=(K // tr,),
        in_specs=[pl.BlockSpec((None, tr, N), lambda i: (l, i, 0))],
        out_specs=pl.BlockSpec((tr, N), lambda i: (i, 0)),
        out_shape=jax.ShapeDtypeStruct((K, N), BF16),
        compiler_params=_cparams(1), name="cast_weight")(w)


def _pack_tokens_kernel(xp_ref, xs_ref, x_ref, xb_ref, *, split_tile):
    @pl.when(pl.program_id(0) < split_tile)
    def _():
        x_ref[...] = xp_ref[...]
        xb_ref[...] = xp_ref[...].astype(BF16)

    @pl.when(pl.program_id(0) >= split_tile)
    def _():
        x_ref[...] = xs_ref[...]
        xb_ref[...] = xs_ref[...].astype(BF16)


def pack_tokens(xp, xs):
    (Np, D), Ns = xp.shape, xs.shape[0]
    tm = _tile(int(np.gcd(Np, Ns)), 512, V7X_BF16_SUBLANES)
    st = Np // tm
    out = pl.BlockSpec((tm, D), lambda i: (i, 0))
    return pl.pallas_call(
        functools.partial(_pack_tokens_kernel, split_tile=st), grid=((Np + Ns) // tm,),
        in_specs=[pl.BlockSpec((tm, D), lambda i: (jnp.minimum(i, st - 1), 0)),
                  pl.BlockSpec((tm, D), lambda i: (jnp.maximum(i - st, 0), 0))],
        out_specs=[out, out],
        out_shape=[jax.ShapeDtypeStruct((Np + Ns, D), F32), jax.ShapeDtypeStruct((Np + Ns, D), BF16)],
        compiler_params=_cparams(1), name="pack_tokens")(xp, xs)


def _matmul_ws_kernel(x_ref, w_ref, o_ref, wb_ref):
    @pl.when(pl.program_id(1) == 0)
    def _():
        wb_ref[...] = w_ref[...].astype(BF16)
    o_ref[...] = jnp.dot(x_ref[...], wb_ref[...], preferred_element_type=F32).astype(o_ref.dtype)


def matmul_ws(x, w, l, out_dtype):
    M, K = x.shape
    N = w.shape[-1]
    tm = _tile(M, 1024, V7X_BF16_SUBLANES)
    tn = _tile(N, 1024, V7X_LANES)
    return pl.pallas_call(
        _matmul_ws_kernel, grid=(N // tn, M // tm),
        in_specs=[pl.BlockSpec((tm, K), lambda j, i: (i, 0)),
                  pl.BlockSpec((None, K, tn), lambda j, i: (l, 0, j))],
        out_specs=pl.BlockSpec((tm, tn), lambda j, i: (i, j)),
        out_shape=jax.ShapeDtypeStruct((M, N), out_dtype),
        scratch_shapes=[pltpu.VMEM((K, tn), BF16)],
        compiler_params=_cparams(2), name="in_proj")(x, w)


def _conv_post(conv_chunks, lg_ref, lb_ref, write, C):
    s = conv_chunks[0]
    for ch in conv_chunks[1:]:
        s = s + ch
    mu = jnp.sum(s, axis=-1, keepdims=True) / C
    sq = None
    for ch in conv_chunks:
        d = ch - mu
        sq = d * d if sq is None else sq + d * d
    var = jnp.sum(sq, axis=-1, keepdims=True) / C
    inv = lax.rsqrt(var + LN_EPS)
    for c, ch in enumerate(conv_chunks):
        sl = slice(c * V7X_LANES, (c + 1) * V7X_LANES)
        z = (ch - mu) * inv * lg_ref[:, sl] + lb_ref[:, sl]
        write(c, _silu(z))


def _glu(a_ref, g_ref):
    return a_ref[...].astype(F32) * jax.nn.sigmoid(g_ref[...].astype(F32))


def _conv_prompt_kernel(a_ref, g_ref, ha_ref, hg_ref, w3_ref, cb3_ref, lg_ref, lb_ref,
                        o_ref, nb_ref, upad_ref, conv_ref, *, tt, K, C, rb):
    i = pl.program_id(1)
    nC = C // V7X_LANES
    off = HIST_ROWS - (K - 1)
    u = _glu(a_ref, g_ref)
    uh = jnp.where(i > 0, _glu(ha_ref, hg_ref), 0.0)
    for c in range(nC):
        sl = slice(c * V7X_LANES, (c + 1) * V7X_LANES)
        upad_ref[c, 0:HIST_ROWS, :] = uh[:, sl]
        upad_ref[c, HIST_ROWS:HIST_ROWS + tt, :] = u[:, sl]

    def chunk_body(c, carry):
        for r in range(tt // rb):
            acc = jnp.zeros((rb, V7X_LANES), F32)
            for j in range(K):
                lo = r * rb + off + j
                acc = acc + upad_ref[c, lo:lo + rb, :] * w3_ref[c, j:j + 1, :]
            conv_ref[c, r * rb:(r + 1) * rb, :] = acc + cb3_ref[c]
        return carry

    lax.fori_loop(0, nC, chunk_body, 0)
    def write(c, v):
        o_ref[:, c * V7X_LANES:(c + 1) * V7X_LANES] = v.astype(o_ref.dtype)

    _conv_post([conv_ref[c] for c in range(nC)], lg_ref, lb_ref, write, C)

    @pl.when(i == pl.num_programs(1) - 1)
    def _():
        for c in range(nC):
            sl = slice(c * V7X_LANES, (c + 1) * V7X_LANES)
            nb_ref[:, sl] = upad_ref[c, HIST_ROWS + tt - (K - 1):HIST_ROWS + tt, :]


def conv_prompt(proj, w3, cb3, lg, lb, B, T, C, K, l, hc, nb_prev):
    M = proj.shape[0]
    L = nb_prev.shape[0]
    tt = _tile(T, 256, HIST_ROWS)
    nT = T // tt
    nC = C // V7X_LANES
    hb = tt // HIST_ROWS
    kern = functools.partial(_conv_prompt_kernel, tt=tt, K=K, C=C, rb=_tile(tt, 64))
    hist = lambda col: (lambda b, i: (jnp.maximum((b * nT + i) * hb - 1, 0), col))
    in_specs = [pl.BlockSpec((tt, C), lambda b, i: (b * nT + i, 0)),
                pl.BlockSpec((tt, C), lambda b, i: (b * nT + i, 1)),
                pl.BlockSpec((HIST_ROWS, C), hist(0)),
                pl.BlockSpec((HIST_ROWS, C), hist(1)),
                pl.BlockSpec((nC, K, V7X_LANES), lambda b, i: (0, 0, 0)),
                pl.BlockSpec((nC, 1, V7X_LANES), lambda b, i: (0, 0, 0)),
                pl.BlockSpec((1, C), lambda b, i: (0, 0)),
                pl.BlockSpec((1, C), lambda b, i: (0, 0))]
    args = [proj, proj, proj, proj, w3, cb3, lg, lb, hc, nb_prev]
    in_specs += [ANY, ANY]
    aliases = {8: 0, 9: 1}
    return pl.pallas_call(
        _skip_refs(kern, 8, 2), grid=(B, nT), in_specs=in_specs,
        out_specs=[pl.BlockSpec((tt, C), lambda b, i: (b * nT + i, 0)),
                   pl.BlockSpec((None, None, K - 1, C), lambda b, i: (l, b, 0, 0))],
        out_shape=[jax.ShapeDtypeStruct((M, C), BF16),
                   jax.ShapeDtypeStruct((L, B, K - 1, C), F32)],
        scratch_shapes=[pltpu.VMEM((nC, HIST_ROWS + tt, V7X_LANES), F32),
                        pltpu.VMEM((nC, tt, V7X_LANES), F32)],
        input_output_aliases=aliases,
        compiler_params=_cparams(2), name="conv_prompt")(*args)


def _conv_sample_kernel(a_ref, g_ref, cache_ref, w_ref, cb_ref, lg_ref, lb_ref,
                        o_ref, nb_ref, u_ref, upad_ref, z_ref, *, T, K, C, bb):
    nC = C // V7X_LANES
    lanes = [slice(c * V7X_LANES, (c + 1) * V7X_LANES) for c in range(nC)]
    u = _glu(a_ref, g_ref)
    for c in range(nC):
        u_ref[c] = u[:, lanes[c]]
    upad_ref[0:K - 1] = cache_ref[...]
    for t in range(T):
        for c in range(nC):
            upad_ref[K - 1 + t, :, lanes[c]] = u_ref[c, pl.ds(t, bb, stride=T), :]
    nb_ref[...] = upad_ref[T:T + K - 1]
    chunks = []
    for c in range(nC):
        acc = jnp.zeros((T, bb, V7X_LANES), F32)
        for j in range(K):
            acc = acc + upad_ref[j:j + T, :, lanes[c]] * w_ref[j:j + 1, lanes[c]][None]
        chunks.append(acc.reshape(T * bb, V7X_LANES) + cb_ref[:, lanes[c]])

    def write(c, v):
        z_ref[c] = v

    _conv_post(chunks, lg_ref, lb_ref, write, C)
    for b in range(0, bb, 2):
        for c in range(nC):
            pair = jnp.concatenate([z_ref[c, pl.ds(b, T, stride=bb), :],
                                    z_ref[c, pl.ds(b + 1, T, stride=bb), :]], axis=0)
            o_ref[b * T:(b + 2) * T, lanes[c]] = pair.astype(o_ref.dtype)


def conv_sample(proj, cache_t, l, w, cb, lg, lb, row0, B, T, C, K, hc, nb_prev):
    assert T % 8 == 0
    L = cache_t.shape[0]
    bb = _tile(B, 16, 8)
    assert row0 % (bb * T) == 0
    blk0 = row0 // (bb * T)
    kern = functools.partial(_conv_sample_kernel, T=T, K=K, C=C, bb=bb)
    vec = pl.BlockSpec((1, C), lambda s: (0, 0))
    in_specs = [pl.BlockSpec((bb * T, C), lambda s: (blk0 + s, 0)),
                pl.BlockSpec((bb * T, C), lambda s: (blk0 + s, 1)),
                pl.BlockSpec((None, K - 1, bb, C), lambda s: (l, 0, s, 0)),
                pl.BlockSpec((K, C), lambda s: (0, 0)), vec, vec, vec, ANY, ANY]
    args = [proj, proj, cache_t, w, cb, lg, lb, hc, nb_prev]
    aliases = {7: 0, 8: 1}
    return pl.pallas_call(
        _skip_refs(kern, 7, 2), grid=(B // bb,), in_specs=in_specs,
        out_specs=[pl.BlockSpec((bb * T, C), lambda s: (blk0 + s, 0)),
                   pl.BlockSpec((None, K - 1, bb, C), lambda s: (l, 0, s, 0))],
        out_shape=[jax.ShapeDtypeStruct(hc.shape, BF16),
                   jax.ShapeDtypeStruct((L, K - 1, B, C), F32)],
        scratch_shapes=[pltpu.VMEM((C // V7X_LANES, bb * T, V7X_LANES), F32),
                        pltpu.VMEM((K - 1 + T, bb, C), F32),
                        pltpu.VMEM((C // V7X_LANES, T * bb, V7X_LANES), F32)],
        input_output_aliases=aliases,
        compiler_params=_cparams(1), name="conv_sample")(*args)


def _retention_tables(H, chunk, rows):
    gamma = (1.0 - 2.0 ** (-5.0 - np.arange(H, dtype=np.float32))).astype(np.float32)
    log_g = np.log(gamma).astype(np.float32)
    idx = np.arange(chunk, dtype=np.float32)
    diff = idx[:, None] - idx[None, :]
    dmask = np.where(diff[None] >= 0, np.exp(diff[None] * log_g[:, None, None]), 0.0).astype(np.float32)
    q_dec = np.exp((idx[None, :] + np.float32(1.0)) * log_g[:, None]).astype(np.float32)
    k_dec = np.exp((np.float32(chunk) - np.float32(1.0) - idx[None, :]) * log_g[:, None]).astype(np.float32)
    c_dec = np.exp(np.float32(chunk) * log_g).astype(np.float32)
    dm = np.zeros((H, rows, rows), np.float32)
    dm[:, :chunk, :chunk] = dmask
    qd = np.zeros((H, rows, 1), np.float32)
    qd[:, :chunk, 0] = q_dec
    kd = np.zeros((H, rows, 1), np.float32)
    kd[:, :chunk, 0] = k_dec
    return dm, qd, kd, [float(c) for c in c_dec]


def _rotary_tables(pos, half):
    inv = (1.0 / (ROPE_BASE ** (np.arange(half, dtype=np.float32) / half))).astype(np.float32)
    ang = (pos.astype(np.float32)[:, None] * inv[None, :]).astype(np.float32)
    return np.cos(ang).astype(np.float32), np.sin(ang).astype(np.float32)


def _retention_kernel(*refs, H, DK, C, nb, cps, rows, c_dec, has_s0):
    if has_s0:
        (q_ref, k_ref, v_ref, g_ref, cos_ref, sin_ref, dm_ref, qd_ref, kd_ref, gng_ref, gnb_ref,
         s0_ref, o_ref, ns_ref) = refs
    else:
        (q_ref, k_ref, v_ref, g_ref, cos_ref, sin_ref, dm_ref, qd_ref, kd_ref, gng_ref, gnb_ref,
         o_ref, ns_ref) = refs
        s0_ref = None
    half = DK // 2
    scale = DK ** -0.5

    @pl.when(pl.program_id(1) == 0)
    def _():
        if has_s0:
            ns_ref[...] = s0_ref[...]
        else:
            ns_ref[...] = jnp.zeros(ns_ref.shape, F32)

    q_all = q_ref[...].astype(F32)
    k_all = k_ref[...].astype(F32)
    v_all = v_ref[...].astype(F32)
    g_all = g_ref[...].astype(F32)

    def rot(x, cos, sin):
        x1, x2 = x[:, :half], x[:, half:]
        return jnp.concatenate([x1 * cos - x2 * sin, x1 * sin + x2 * cos], axis=-1)

    def pad_rows(x):
        if rows == C:
            return x
        return jnp.concatenate([x, jnp.zeros((rows - C, x.shape[1]), x.dtype)], axis=0)

    outs = []
    for n in range(nb * cps):
        s, cc = divmod(n, cps)
        rs = slice(n * C, (n + 1) * C)
        cos = cos_ref[cc * C:(cc + 1) * C, :]
        sin = sin_ref[cc * C:(cc + 1) * C, :]
        heads = []
        for h in range(H):
            hs = slice(h * DK, (h + 1) * DK)
            qr = pad_rows(rot(q_all[rs, hs], cos, sin))
            kr = pad_rows(rot(k_all[rs, hs], cos, sin) * scale)
            vb = pad_rows(v_all[rs, hs]).astype(BF16)
            S = ns_ref[s, h]
            scores = lax.dot_general(qr.astype(BF16), kr.astype(BF16), (((1,), (1,)), ((), ())),
                                     preferred_element_type=F32) * dm_ref[h]
            o = (jnp.dot(scores.astype(BF16), vb, preferred_element_type=F32)
                 + jnp.dot((qr * qd_ref[h]).astype(BF16), S.astype(BF16), preferred_element_type=F32))
            kd = (kr * kd_ref[h]).astype(BF16)
            ns_ref[s, h] = c_dec[h] * S + lax.dot_general(kd, vb, (((0,), (0,)), ((), ())),
                                                          preferred_element_type=F32)
            o = o[:C]
            mu = jnp.mean(o, axis=-1, keepdims=True)
            d = o - mu
            var = jnp.mean(d * d, axis=-1, keepdims=True)
            on = d * lax.rsqrt(var + LN_EPS) * gng_ref[:, hs] + gnb_ref[:, hs]
            heads.append(_silu(g_all[rs, hs]) * on)
        outs.append(jnp.concatenate(heads, axis=-1))
    o_ref[...] = jnp.concatenate(outs, axis=0).astype(o_ref.dtype)


def retention(proj, gng, gnb, row0, B, T, H, DK, pos0, l, hr, ns_prev, state=None):
    M = proj.shape[0]
    L = ns_prev.shape[0]
    R = H * DK
    C = int(np.gcd(T, RET_CHUNK))
    n_chunks = T // C
    nb = 1 if n_chunks > 1 else _tile(B, RET_SEQS_PER_STEP, 1)
    cps = _tile(n_chunks, RET_CHUNKS_PER_STEP, 1)
    n_steps = n_chunks // cps
    rows = max(C, V7X_BF16_SUBLANES)
    dm, qd, kd, c_dec = _retention_tables(H, C, rows)
    cos, sin = _rotary_tables(pos0 + np.arange(T), DK // 2)
    blk_rows = nb * cps * C
    assert row0 % blk_rows == 0 and (blk_rows % V7X_BF16_SUBLANES == 0)
    blk0 = row0 // blk_rows
    has_s0 = state is not None
    kern = functools.partial(_retention_kernel, H=H, DK=DK, C=C, nb=nb, cps=cps, rows=rows,
                             c_dec=c_dec, has_s0=has_s0)

    def row_map(col):
        return lambda sb, c: (blk0 + sb * n_steps + c, col)

    in_specs = [pl.BlockSpec((blk_rows, R), row_map(2)),
                pl.BlockSpec((blk_rows, R), row_map(3)),
                pl.BlockSpec((blk_rows, R), row_map(4)),
                pl.BlockSpec((blk_rows, R), row_map(5)),
                pl.BlockSpec((cps * C, DK // 2), lambda sb, c: (c, 0)),
                pl.BlockSpec((cps * C, DK // 2), lambda sb, c: (c, 0)),
                pl.BlockSpec((H, rows, rows), lambda sb, c: (0, 0, 0)),
                pl.BlockSpec((H, rows, 1), lambda sb, c: (0, 0, 0)),
                pl.BlockSpec((H, rows, 1), lambda sb, c: (0, 0, 0)),
                pl.BlockSpec((1, R), lambda sb, c: (0, 0)),
                pl.BlockSpec((1, R), lambda sb, c: (0, 0))]
    args = [proj, proj, proj, proj, jnp.asarray(cos), jnp.asarray(sin), jnp.asarray(dm),
            jnp.asarray(qd), jnp.asarray(kd), gng, gnb]
    if has_s0:
        in_specs.append(pl.BlockSpec((None, nb, H, DK, DK), lambda sb, c: (l, sb, 0, 0, 0)))
        args.append(state)
    n_used = len(args)
    aliases = {n_used: 0, n_used + 1: 1}
    in_specs += [ANY, ANY]
    args += [hr, ns_prev]
    return pl.pallas_call(
        _skip_refs(kern, n_used, 2), grid=(B // nb, n_steps),
        in_specs=in_specs,
        out_specs=[pl.BlockSpec((blk_rows, R), lambda sb, c: (blk0 + sb * n_steps + c, 0)),
                   pl.BlockSpec((None, nb, H, DK, DK), lambda sb, c: (l, sb, 0, 0, 0))],
        out_shape=[jax.ShapeDtypeStruct((M, R), BF16),
                   jax.ShapeDtypeStruct((L, B, H, DK, DK), F32)],
        input_output_aliases=aliases,
        compiler_params=_cparams(2), name="retention")(*args)


def _outproj_ln_kernel(hc_ref, hr_ref, x_ref, w_ref, g_ref, b_ref, xo_ref, xb_ref, *, alpha, cc, sub):
    for s in range(x_ref.shape[0] // sub):
        rs = slice(s * sub, (s + 1) * sub)
        m = (jnp.dot(hc_ref[rs, :], w_ref[0:cc, :], preferred_element_type=F32)
             + jnp.dot(hr_ref[rs, :], w_ref[cc:, :], preferred_element_type=F32))
        z = _layer_norm(alpha * x_ref[rs, :] + m, g_ref[...], b_ref[...])
        xo_ref[rs, :] = z
        xb_ref[rs, :] = z.astype(BF16)


def outproj_ln(hc, hr, x, wb, g, b, alpha):
    M, D = x.shape
    cc = hc.shape[1]
    tm = _tile(M, 512, V7X_BF16_SUBLANES)
    kern = functools.partial(_outproj_ln_kernel, alpha=alpha, cc=cc, sub=_tile(tm, 128, V7X_BF16_SUBLANES))
    return pl.pallas_call(
        kern, grid=(M // tm,),
        in_specs=[pl.BlockSpec((tm, cc), lambda i: (i, 0)),
                  pl.BlockSpec((tm, hr.shape[1]), lambda i: (i, 0)),
                  pl.BlockSpec((tm, D), lambda i: (i, 0)),
                  pl.BlockSpec(wb.shape, lambda i: (0, 0)),
                  pl.BlockSpec((1, D), lambda i: (0, 0)),
                  pl.BlockSpec((1, D), lambda i: (0, 0))],
        out_specs=[pl.BlockSpec((tm, D), lambda i: (i, 0)),
                   pl.BlockSpec((tm, D), lambda i: (i, 0))],
        out_shape=[jax.ShapeDtypeStruct((M, D), F32), jax.ShapeDtypeStruct((M, D), BF16)],
        compiler_params=_cparams(1), name="outproj_ln")(hc, hr, x, wb, g, b)


def _ln_row_tile(M, n_first, pref):
    n = M if n_first is None else int(np.gcd(n_first, M - n_first))
    return _tile(n, pref, V7X_BF16_SUBLANES)


def _ln_outputs(M, D, tm, n_first, wrap):
    if n_first is None:
        row = pl.BlockSpec((tm, D), wrap(lambda i: (i, 0)))
        return [row, row], [jax.ShapeDtypeStruct((M, D), F32), jax.ShapeDtypeStruct((M, D), BF16)], None
    st = n_first // tm
    specs = [pl.BlockSpec((tm, D), wrap(lambda i: (jnp.minimum(i, st - 1), 0))),
             pl.BlockSpec((tm, D), wrap(lambda i: (jnp.maximum(i - st, 0), 0)))]
    return specs, [jax.ShapeDtypeStruct((n_first, D), F32), jax.ShapeDtypeStruct((M - n_first, D), F32)], st


def _ln_emit(z, o0_ref, o1_ref, split_tile):
    if split_tile is None:
        o0_ref[...] = z
        o1_ref[...] = z.astype(BF16)
    else:
        @pl.when(pl.program_id(0) < split_tile)
        def _():
            o0_ref[...] = z

        @pl.when(pl.program_id(0) >= split_tile)
        def _():
            o1_ref[...] = z


def _resid_ln_kernel(f_ref, x_ref, g_ref, b_ref, o0_ref, o1_ref, *, alpha, split_tile):
    z = _layer_norm(alpha * x_ref[...] + f_ref[...], g_ref[...], b_ref[...])
    _ln_emit(z, o0_ref, o1_ref, split_tile)


def resid_ln(f, x, g, b, alpha, n_first=None):
    M, D = x.shape
    tm = _ln_row_tile(M, n_first, 512)
    out_specs, out_shape, st = _ln_outputs(M, D, tm, n_first, lambda fn: fn)
    kern = functools.partial(_resid_ln_kernel, alpha=alpha, split_tile=st)
    row = pl.BlockSpec((tm, D), lambda i: (i, 0))
    vec = pl.BlockSpec((1, D), lambda i: (0, 0))
    return pl.pallas_call(
        kern, grid=(M // tm,), in_specs=[row, row, vec, vec], out_specs=out_specs, out_shape=out_shape,
        compiler_params=_cparams(1), name="resid_ln")(f, x, g, b)


def _tile_state(te_ref, na_ref):
    i = pl.program_id(1)
    active = i < na_ref[0]
    changed = jnp.logical_or(i == 0, te_ref[i] != te_ref[jnp.maximum(i - 1, 0)])
    return active, changed


def _weight_stream(te_ref, na_ref, tend_ref, copies, cast):
    j = pl.program_id(0)
    i = pl.program_id(1)
    active, changed = _tile_state(te_ref, na_ref)

    @pl.when(jnp.logical_and(active, changed))
    def _():
        e = te_ref[i]

        @pl.when(jnp.logical_and(j == 0, i == 0))
        def _():
            for c in copies(e, j):
                c.start()
        for c in copies(e, j):
            c.wait()
        cast()
        nxt = tend_ref[e]
        same_j = nxt < na_ref[0]
        e2 = te_ref[jnp.where(same_j, nxt, 0)]
        j2 = jnp.where(same_j, j, j + 1)

        @pl.when(j2 < pl.num_programs(0))
        def _():
            for c in copies(e2, j2):
                c.start()
    return active


def _row_tile_compute(rows, tm, sub, compute, o_ref):
    @pl.when(rows == tm)
    def _():
        o_ref[...] = compute(0, tm)

    @pl.when(rows < tm)
    def _():
        for s in range(tm // sub):
            @pl.when(s * sub < rows)
            def _():
                o_ref[s * sub:(s + 1) * sub, :] = compute(s * sub, sub)

            @pl.when(s * sub >= rows)
            def _():
                o_ref[s * sub:(s + 1) * sub, :] = jnp.zeros((sub, o_ref.shape[1]), o_ref.dtype)


def _gateup_kernel(te_ref, na_ref, tend_ref, tr_ref, x_ref, wg_hbm, wu_hbm, o_ref,
                   sg_ref, su_ref, wgb_ref, wub_ref, sem, *, li, tn, sub):
    def copies(e, j):
        cols = pl.ds(pl.multiple_of(j * tn, tn), tn)
        return (pltpu.make_async_copy(wg_hbm.at[li, e, :, cols], sg_ref, sem.at[0]),
                pltpu.make_async_copy(wu_hbm.at[li, e, :, cols], su_ref, sem.at[1]))

    def cast():
        wgb_ref[...] = sg_ref[...].astype(BF16)
        wub_ref[...] = su_ref[...].astype(BF16)

    _weight_stream(te_ref, na_ref, tend_ref, copies, cast)

    def compute(r0, n):
        x = x_ref[r0:r0 + n, :]
        gate = jnp.dot(x, wgb_ref[...], preferred_element_type=F32)
        up = jnp.dot(x, wub_ref[...], preferred_element_type=F32)
        return (_silu(gate) * up).astype(o_ref.dtype)

    _row_tile_compute(tr_ref[pl.program_id(1)], x_ref.shape[0], sub, compute, o_ref)


def _down_kernel(te_ref, na_ref, tend_ref, tr_ref, h_ref, w_hbm, o_ref, s_ref, wb_ref, sem, *, li, tn, sub):
    def copies(e, j):
        cols = pl.ds(pl.multiple_of(j * tn, tn), tn)
        return (pltpu.make_async_copy(w_hbm.at[li, e, :, cols], s_ref, sem.at[0]),)

    def cast():
        wb_ref[...] = s_ref[...].astype(BF16)

    _weight_stream(te_ref, na_ref, tend_ref, copies, cast)

    def compute(r0, n):
        return jnp.dot(h_ref[r0:r0 + n, :], wb_ref[...], preferred_element_type=F32).astype(o_ref.dtype)

    _row_tile_compute(tr_ref[pl.program_id(1)], h_ref.shape[0], sub, compute, o_ref)


def grouped_swiglu(xs, wg, wu, wd, li, tile_expert, n_active, tile_end, tile_rows, tm, out_dtype):
    S, D = xs.shape
    Fd = wg.shape[-1]
    nT = S // tm
    tn = _tile(Fd, 512, V7X_LANES)
    sub = _tile(tm, 128, V7X_BF16_SUBLANES)

    def row_map(j, i, te, na, tend, tr):
        return (jnp.minimum(i, na[0] - 1), 0)

    def out_map(j, i, te, na, tend, tr):
        return (i, j)

    h = pl.pallas_call(
        functools.partial(_gateup_kernel, li=li, tn=tn, sub=sub),
        grid_spec=pltpu.PrefetchScalarGridSpec(
            num_scalar_prefetch=4, grid=(Fd // tn, nT),
            in_specs=[pl.BlockSpec((tm, D), row_map), ANY, ANY],
            out_specs=pl.BlockSpec((tm, tn), out_map),
            scratch_shapes=[pltpu.VMEM((D, tn), F32), pltpu.VMEM((D, tn), F32),
                            pltpu.VMEM((D, tn), BF16), pltpu.VMEM((D, tn), BF16),
                            pltpu.SemaphoreType.DMA((2,))]),
        out_shape=jax.ShapeDtypeStruct((S, Fd), BF16),
        compiler_params=_cparams(2), name="ffn_gate_up")(tile_expert, n_active, tile_end, tile_rows, xs, wg, wu)

    tn2 = _tile(D, 512, V7X_LANES)
    return pl.pallas_call(
        functools.partial(_down_kernel, li=li, tn=tn2, sub=sub),
        grid_spec=pltpu.PrefetchScalarGridSpec(
            num_scalar_prefetch=4, grid=(D // tn2, nT),
            in_specs=[pl.BlockSpec((tm, Fd), row_map), ANY],
            out_specs=pl.BlockSpec((tm, tn2), out_map),
            scratch_shapes=[pltpu.VMEM((Fd, tn2), F32), pltpu.VMEM((Fd, tn2), BF16),
                            pltpu.SemaphoreType.DMA((1,))]),
        out_shape=jax.ShapeDtypeStruct((S, D), out_dtype),
        compiler_params=_cparams(2), name="ffn_down")(tile_expert, n_active, tile_end, tile_rows, h, wd)


def _router_kernel(x_ref, w_ref, b_ref, idx_ref, gate_ref):
    x = x_ref[...]
    w = w_ref[...]
    xh = x.astype(BF16)
    xl = (x - xh.astype(F32)).astype(BF16)
    wh = w.astype(BF16)
    wl = (w - wh.astype(F32)).astype(BF16)
    logits = (jnp.dot(xh, wh, preferred_element_type=F32) + jnp.dot(xh, wl, preferred_element_type=F32)
              + jnp.dot(xl, wh, preferred_element_type=F32)) + b_ref[...]
    col = lax.broadcasted_iota(jnp.int32, logits.shape, 1)
    big = jnp.int32(logits.shape[1])
    m1 = jnp.max(logits, axis=-1, keepdims=True)
    i1 = jnp.min(jnp.where(logits == m1, col, big), axis=-1, keepdims=True)
    rest = jnp.where(col == i1, -jnp.inf, logits)
    m2 = jnp.max(rest, axis=-1, keepdims=True)
    i2 = jnp.min(jnp.where(rest == m2, col, big), axis=-1, keepdims=True)
    e = jnp.exp(m2 - m1)
    g1 = 1.0 / (1.0 + e)
    g2 = e / (1.0 + e)
    idx_ref[...] = jnp.where(col == 0, i1, jnp.where(col == 1, i2, 0))
    gate_ref[...] = jnp.where(col == 0, g1, jnp.where(col == 1, g2, 0.0))


def router(x, w_pad, b_pad):
    M, D = x.shape
    tm = _tile(M, 512)
    out = pl.BlockSpec((tm, V7X_LANES), lambda i: (i, 0))
    return pl.pallas_call(
        _router_kernel, grid=(M // tm,),
        in_specs=[pl.BlockSpec((tm, D), lambda i: (i, 0)),
                  pl.BlockSpec((D, V7X_LANES), lambda i: (0, 0)),
                  pl.BlockSpec((1, V7X_LANES), lambda i: (0, 0))],
        out_specs=[out, out],
        out_shape=[jax.ShapeDtypeStruct((M, V7X_LANES), jnp.int32),
                   jax.ShapeDtypeStruct((M, V7X_LANES), F32)],
        compiler_params=_cparams(1), name="router")(x, w_pad, b_pad)


ISSUE_UNROLL = 8
ISSUE_STRIDE = 37


def _dispatch_kernel(src_ref, na_ref, x_hbm, o_ref, buf_ref, sem, *, R):
    i = pl.program_id(0)
    na = na_ref[0]

    def issue_tile(t, slot):
        def body(n, carry):
            r = (n * ISSUE_STRIDE) % R if R & (R - 1) == 0 else n
            tok = src_ref[t * R + r]
            pltpu.make_async_copy(x_hbm.at[pl.ds(tok, 1), :], buf_ref.at[slot, pl.ds(r, 1), :],
                                  sem.at[slot]).start()
            return carry
        lax.fori_loop(0, R, body, 0, unroll=ISSUE_UNROLL)

    @pl.when(i == 0)
    def _():
        issue_tile(0, 0)

    @pl.when(i + 1 < na)
    def _():
        issue_tile(i + 1, (i + 1) % 2)

    @pl.when(i < na)
    def _():
        slot = i % 2
        pltpu.make_async_copy(x_hbm.at[pl.ds(0, R), :], buf_ref.at[slot], sem.at[slot]).wait()
        o_ref[...] = buf_ref[slot].astype(o_ref.dtype)

    @pl.when(i >= na)
    def _():
        o_ref[...] = jnp.zeros(o_ref.shape, o_ref.dtype)


def dispatch(x, src, n_active, S, R):
    M, D = x.shape
    kern = functools.partial(_dispatch_kernel, R=R)
    return pl.pallas_call(
        kern,
        grid_spec=pltpu.PrefetchScalarGridSpec(
            num_scalar_prefetch=2, grid=(S // R,),
            in_specs=[ANY],
            out_specs=pl.BlockSpec((R, D), lambda i, src, na: (i, 0)),
            scratch_shapes=[pltpu.VMEM((2, R, D), F32), pltpu.SemaphoreType.DMA((2,))]),
        out_shape=jax.ShapeDtypeStruct((S, D), BF16),
        compiler_params=_cparams(1), name="moe_dispatch")(src, n_active, x)


def _combine_ln_kernel(pos_ref, ys_hbm, x_ref, gate_ref, g_ref, b_ref, o0_ref, o1_ref, buf_ref, sem,
                       *, R, alpha, split_tile):
    i = pl.program_id(0)

    def issue_tile(t, slot):
        def body(r, carry):
            a = (t * R + r) * TOP_K
            for k in range(TOP_K):
                pltpu.make_async_copy(ys_hbm.at[pl.ds(pos_ref[a + k], 1), :],
                                      buf_ref.at[slot, k, pl.ds(r, 1), :], sem.at[slot]).start()
            return carry
        lax.fori_loop(0, R, body, 0, unroll=ISSUE_UNROLL // TOP_K)

    @pl.when(i == 0)
    def _():
        issue_tile(0, 0)

    @pl.when(i + 1 < pl.num_programs(0))
    def _():
        issue_tile(i + 1, (i + 1) % 2)

    slot = i % 2
    for k in range(TOP_K):
        pltpu.make_async_copy(ys_hbm.at[pl.ds(0, R), :], buf_ref.at[slot, k], sem.at[slot]).wait()
    gates = gate_ref[...]
    y = gates[:, 0:1] * buf_ref[slot, 0]
    for k in range(1, TOP_K):
        y = y + gates[:, k:k + 1] * buf_ref[slot, k]
    z = _layer_norm(alpha * x_ref[...] + y, g_ref[...], b_ref[...])
    _ln_emit(z, o0_ref, o1_ref, split_tile)


def combine_ln(ys, pos, x, gates, g, b, alpha, n_first=None):
    M, D = x.shape
    R = _ln_row_tile(M, n_first, 256)
    out_specs, out_shape, st = _ln_outputs(M, D, R, n_first, lambda fn: (lambda i, pos: fn(i)))
    kern = functools.partial(_combine_ln_kernel, R=R, alpha=alpha, split_tile=st)
    row = lambda i, pos: (i, 0)
    fix = lambda i, pos: (0, 0)
    return pl.pallas_call(
        kern,
        grid_spec=pltpu.PrefetchScalarGridSpec(
            num_scalar_prefetch=1, grid=(M // R,),
            in_specs=[ANY,
                      pl.BlockSpec((R, D), row),
                      pl.BlockSpec((R, V7X_LANES), row),
                      pl.BlockSpec((1, D), fix),
                      pl.BlockSpec((1, D), fix)],
            out_specs=out_specs,
            scratch_shapes=[pltpu.VMEM((2, TOP_K, R, D), F32), pltpu.SemaphoreType.DMA((2,))]),
        out_shape=out_shape,
        compiler_params=_cparams(1), name="moe_combine_ln")(pos, ys, x, gates, g, b)


def _routing_plan(idx, E, tm, n_tiles):
    e_flat = idx.reshape(-1)
    A = e_flat.shape[0]
    a_ids = jnp.arange(A, dtype=jnp.int32)
    order = jnp.sort(e_flat * A + a_ids) % A
    rank = jnp.argsort(order).astype(jnp.int32)
    counts = jnp.sum((e_flat[:, None] == jnp.arange(E, dtype=jnp.int32)[None, :]).astype(jnp.int32), axis=0)
    tiles_e = (counts + tm - 1) // tm
    tile_end = jnp.cumsum(tiles_e).astype(jnp.int32)
    tile_start = tile_end - tiles_e
    pad_start = tile_start * tm
    start = jnp.cumsum(counts) - counts
    n_active = tile_end[-1]
    pos = pad_start[e_flat] + rank - start[e_flat]
    tile_id = jnp.arange(n_tiles, dtype=jnp.int32)
    te = jnp.sum((jnp.minimum(tile_id, n_active - 1)[:, None] >= tile_end[None, :]).astype(jnp.int32), axis=1)
    tile_rows = jnp.where(tile_id < n_active,
                          jnp.clip(counts[te] - (tile_id - tile_start[te]) * tm, 0, tm), 0)
    slot = jnp.arange(n_tiles * tm, dtype=jnp.int32)
    se = te[slot // tm]
    off = slot - pad_start[se]
    r = start[se] + jnp.clip(off, 0, jnp.maximum(counts[se] - 1, 0))
    src = order[jnp.clip(r, 0, A - 1)] // TOP_K
    return (src.astype(jnp.int32), pos.astype(jnp.int32), te,
            n_active.reshape(1).astype(jnp.int32), tile_end, tile_rows.astype(jnp.int32))


def kernel(x_prompt, x_sample, cache_conv, state_ret, w_in, conv_w, conv_b, conv_ln_g, conv_ln_b,
           ret_gn_g, ret_gn_b, w_out, ln1_g, ln1_b, ln2_g, ln2_b, w_ff_gate, w_ff_up, w_ff_down,
           w_router, b_router, w_exp_gate, w_exp_up, w_exp_down):
    B, T, D = x_prompt.shape
    Bs, Ts, _ = x_sample.shape
    L = w_in.shape[0]
    K = conv_w.shape[1]
    C = conv_w.shape[2]
    H, DK = state_ret.shape[2], state_ret.shape[3]
    E = w_router.shape[-1]
    Np, Ns = B * T, Bs * Ts
    M = Np + Ns
    nC = C // V7X_LANES
    alpha = float((2 * L) ** 0.25)
    assert w_in.shape[-1] == 6 * C and H * DK == C and D == 2 * C

    x, xb = pack_tokens(x_prompt.reshape(Np, D), x_sample.reshape(Ns, D))

    tm_dense = _tile(M, 1024, V7X_BF16_SUBLANES)
    n_dense_tiles = M // tm_dense
    dense_te = jnp.zeros((n_dense_tiles,), jnp.int32)
    dense_na = jnp.full((1,), n_dense_tiles, jnp.int32)
    dense_rows = jnp.full((n_dense_tiles,), tm_dense, jnp.int32)
    w_ff_gate4, w_ff_up4, w_ff_down4 = w_ff_gate[:, None], w_ff_up[:, None], w_ff_down[:, None]
    moe_tm = _tile(M * TOP_K, MOE_TILE, V7X_BF16_SUBLANES)
    moe_tiles = (M * TOP_K) // moe_tm + E

    hc = jnp.zeros((M, C), BF16)
    hr = jnp.zeros((M, H * DK), BF16)
    conv_p = jnp.zeros((L, B, K - 1, C), F32)
    conv_s = jnp.zeros((L, K - 1, Bs, C), F32)
    cache_t = cache_conv.transpose(0, 2, 1, 3)
    ret_p = jnp.zeros((L, B, H, DK, DK), F32)
    ret_s = jnp.zeros((L, Bs, H, DK, DK), F32)
    for l in range(L):
        proj = matmul_ws(xb, w_in, l, BF16)

        w3 = conv_w[l].reshape(K, nC, V7X_LANES).transpose(1, 0, 2)
        cb3 = conv_b[l].reshape(nC, 1, V7X_LANES)
        cb, lg, lb = conv_b[l][None], conv_ln_g[l][None], conv_ln_b[l][None]
        hc, conv_p = conv_prompt(proj, w3, cb3, lg, lb, B, T, C, K, l, hc, conv_p)
        hc, conv_s = conv_sample(proj, cache_t, l, conv_w[l], cb, lg, lb, Np, Bs, Ts, C, K, hc, conv_s)

        gng, gnb = ret_gn_g[l][None], ret_gn_b[l][None]
        hr, ret_p = retention(proj, gng, gnb, 0, B, T, H, DK, 0, l, hr, ret_p)
        hr, ret_s = retention(proj, gng, gnb, Np, Bs, Ts, H, DK, PAST_LEN, l, hr, ret_s, state=state_ret)

        x, xb = outproj_ln(hc, hr, x, cast_weight(w_out, l), ln1_g[l][None], ln1_b[l][None], alpha)
        n_first = Np if l == L - 1 else None

        if l % 2 == 0:
            f = grouped_swiglu(xb, w_ff_gate4, w_ff_up4, w_ff_down4, l // 2, dense_te, dense_na,
                               dense_na, dense_rows, tm_dense, BF16)
            x, xb = resid_ln(f, x, ln2_g[l][None], ln2_b[l][None], alpha, n_first)
        else:
            li = l // 2
            w_pad = jnp.zeros((D, V7X_LANES), F32).at[:, :E].set(w_router[li])
            b_pad = jnp.full((1, V7X_LANES), -1e30, F32).at[0, :E].set(b_router[li])
            idx_w, gate_w = router(x, w_pad, b_pad)
            src, pos, te, na, tend, trows = _routing_plan(idx_w[:, :TOP_K], E, moe_tm, moe_tiles)
            xs = dispatch(x, src, na, moe_tiles * moe_tm, moe_tm)
            ys = grouped_swiglu(xs, w_exp_gate, w_exp_up, w_exp_down, li, te, na, tend, trows, moe_tm, F32)
            x, xb = combine_ln(ys, pos, x, gate_w, ln2_g[l][None], ln2_b[l][None], alpha, n_first)

    return (x.reshape(B, T, D), xb.reshape(Bs, Ts, D), conv_p, ret_p, conv_s.transpose(0, 2, 1, 3), ret_s)
```

```python
import functools

import numpy as np
import jax
import jax.numpy as jnp
from jax import lax
from jax.experimental import pallas as pl
from jax.experimental.pallas import tpu as pltpu

F32 = jnp.float32
BF16 = jnp.bfloat16

LN_EPS = 1e-5
ROPE_BASE = 10000.0
RET_CHUNK = 128
PAST_LEN = 16384
TOP_K = 2

V7X_LANES = 128
V7X_BF16_SUBLANES = 16
V7X_VMEM_LIMIT_BYTES = 56 * 1024 * 1024

HIST_ROWS = 32
MOE_TILE = 512
RET_SEQS_PER_STEP = 8
RET_CHUNKS_PER_STEP = 4
ANY = pl.BlockSpec(memory_space=pl.ANY)


def _cparams(n_axes):
    return pltpu.CompilerParams(dimension_semantics=("arbitrary",) * n_axes,
                                vmem_limit_bytes=V7X_VMEM_LIMIT_BYTES)


def _tile(n, pref, mult=8):
    t = min(pref, n)
    while t > mult and (n % t or t % mult):
        t -= mult
    assert n % t == 0 and t % mult == 0, (n, pref, mult)
    return t


def _skip_refs(kern, start, count):
    def wrapped(*refs):
        return kern(*refs[:start], *refs[start + count:])
    return wrapped


def _layer_norm(y, g, b):
    mu = jnp.mean(y, axis=-1, keepdims=True)
    d = y - mu
    var = jnp.mean(d * d, axis=-1, keepdims=True)
    return d * lax.rsqrt(var + LN_EPS) * g + b


def _silu(z):
    return z * jax.nn.sigmoid(z)


def _cast_kernel(w_ref, o_ref):
    o_ref[...] = w_ref[...].astype(o_ref.dtype)


def cast_weight(w, l):
    _, K, N = w.shape
    tr = _tile(K, 512, V7X_BF16_SUBLANES)
    return pl.pallas_call(
        _cast_kernel, grid=(K // tr,),
        in_specs=[pl.BlockSpec((None, tr, N), lambda i: (l, i, 0))],
        out_specs=pl.BlockSpec((tr, N), lambda i: (i, 0)),
        out_shape=jax.ShapeDtypeStruct((K, N), BF16),
        compiler_params=_cparams(1), name="cast_weight")(w)


def _pack_tokens_kernel(xp_ref, xs_ref, x_ref, xb_ref, *, split_tile):
    @pl.when(pl.program_id(0) < split_tile)
    def _():
        x_ref[...] = xp_ref[...]
        xb_ref[...] = xp_ref[...].astype(BF16)

    @pl.when(pl.program_id(0) >= split_tile)
    def _():
        x_ref[...] = xs_ref[...]
        xb_ref[...] = xs_ref[...].astype(BF16)


def pack_tokens(xp, xs):
    (Np, D), Ns = xp.shape, xs.shape[0]
    tm = _tile(int(np.gcd(Np, Ns)), 512, V7X_BF16_SUBLANES)
    st = Np // tm
    out = pl.BlockSpec((tm, D), lambda i: (i, 0))
    return pl.pallas_call(
        functools.partial(_pack_tokens_kernel, split_tile=st), grid=((Np + Ns) // tm,),
        in_specs=[pl.BlockSpec((tm, D), lambda i: (jnp.minimum(i, st - 1), 0)),
                  pl.BlockSpec((tm, D), lambda i: (jnp.maximum(i - st, 0), 0))],
        out_specs=[out, out],
        out_shape=[jax.ShapeDtypeStruct((Np + Ns, D), F32), jax.ShapeDtypeStruct((Np + Ns, D), BF16)],
        compiler_params=_cparams(1), name="pack_tokens")(xp, xs)


def _matmul_ws_kernel(x_ref, w_ref, o_ref, wb_ref):
    @pl.when(pl.program_id(1) == 0)
    def _():
        wb_ref[...] = w_ref[...].astype(BF16)
    o_ref[...] = jnp.dot(x_ref[...], wb_ref[...], preferred_element_type=F32).astype(o_ref.dtype)


def matmul_ws(x, w, l, out_dtype):
    M, K = x.shape
    N = w.shape[-1]
    tm = _tile(M, 1024, V7X_BF16_SUBLANES)
    tn = _tile(N, 1024, V7X_LANES)
    return pl.pallas_call(
        _matmul_ws_kernel, grid=(N // tn, M // tm),
        in_specs=[pl.BlockSpec((tm, K), lambda j, i: (i, 0)),
                  pl.BlockSpec((None, K, tn), lambda j, i: (l, 0, j))],
        out_specs=pl.BlockSpec((tm, tn), lambda j, i: (i, j)),
        out_shape=jax.ShapeDtypeStruct((M, N), out_dtype),
        scratch_shapes=[pltpu.VMEM((K, tn), BF16)],
        compiler_params=_cparams(2), name="in_proj")(x, w)


def _conv_post(conv_chunks, lg_ref, lb_ref, write, C):
    s = conv_chunks[0]
    for ch in conv_chunks[1:]:
        s = s + ch
    mu = jnp.sum(s, axis=-1, keepdims=True) / C
    sq = None
    for ch in conv_chunks:
        d = ch - mu
        sq = d * d if sq is None else sq + d * d
    var = jnp.sum(sq, axis=-1, keepdims=True) / C
    inv = lax.rsqrt(var + LN_EPS)
    for c, ch in enumerate(conv_chunks):
        sl = slice(c * V7X_LANES, (c + 1) * V7X_LANES)
        z = (ch - mu) * inv * lg_ref[:, sl] + lb_ref[:, sl]
        write(c, _silu(z))


def _glu(a_ref, g_ref):
    return a_ref[...].astype(F32) * jax.nn.sigmoid(g_ref[...].astype(F32))


def _conv_prompt_kernel(a_ref, g_ref, ha_ref, hg_ref, w3_ref, cb3_ref, lg_ref, lb_ref,
                        o_ref, nb_ref, upad_ref, conv_ref, *, tt, K, C, rb):
    i = pl.program_id(1)
    nC = C // V7X_LANES
    off = HIST_ROWS - (K - 1)
    u = _glu(a_ref, g_ref)
    uh = jnp.where(i > 0, _glu(ha_ref, hg_ref), 0.0)
    for c in range(nC):
        sl = slice(c * V7X_LANES, (c + 1) * V7X_LANES)
        upad_ref[c, 0:HIST_ROWS, :] = uh[:, sl]
        upad_ref[c, HIST_ROWS:HIST_ROWS + tt, :] = u[:, sl]

    def chunk_body(c, carry):
        for r in range(tt // rb):
            acc = jnp.zeros((rb, V7X_LANES), F32)
            for j in range(K):
                lo = r * rb + off + j
                acc = acc + upad_ref[c, lo:lo + rb, :] * w3_ref[c, j:j + 1, :]
            conv_ref[c, r * rb:(r + 1) * rb, :] = acc + cb3_ref[c]
        return carry

    lax.fori_loop(0, nC, chunk_body, 0)
    def write(c, v):
        o_ref[:, c * V7X_LANES:(c + 1) * V7X_LANES] = v.astype(o_ref.dtype)

    _conv_post([conv_ref[c] for c in range(nC)], lg_ref, lb_ref, write, C)

    @pl.when(i == pl.num_programs(1) - 1)
    def _():
        for c in range(nC):
            sl = slice(c * V7X_LANES, (c + 1) * V7X_LANES)
            nb_ref[:, sl] = upad_ref[c, HIST_ROWS + tt - (K - 1):HIST_ROWS + tt, :]


def conv_prompt(proj, w3, cb3, lg, lb, B, T, C, K, l, hc, nb_prev):
    M = proj.shape[0]
    L = nb_prev.shape[0]
    tt = _tile(T, 256, HIST_ROWS)
    nT = T // tt
    nC = C // V7X_LANES
    hb = tt // HIST_ROWS
    kern = functools.partial(_conv_prompt_kernel, tt=tt, K=K, C=C, rb=_tile(tt, 64))
    hist = lambda col: (lambda b, i: (jnp.maximum((b * nT + i) * hb - 1, 0), col))
    in_specs = [pl.BlockSpec((tt, C), lambda b, i: (b * nT + i, 0)),
                pl.BlockSpec((tt, C), lambda b, i: (b * nT + i, 1)),
                pl.BlockSpec((HIST_ROWS, C), hist(0)),
                pl.BlockSpec((HIST_ROWS, C), hist(1)),
                pl.BlockSpec((nC, K, V7X_LANES), lambda b, i: (0, 0, 0)),
                pl.BlockSpec((nC, 1, V7X_LANES), lambda b, i: (0, 0, 0)),
                pl.BlockSpec((1, C), lambda b, i: (0, 0)),
                pl.BlockSpec((1, C), lambda b, i: (0, 0))]
    args = [proj, proj, proj, proj, w3, cb3, lg, lb, hc, nb_prev]
    in_specs += [ANY, ANY]
    aliases = {8: 0, 9: 1}
    return pl.pallas_call(
        _skip_refs(kern, 8, 2), grid=(B, nT), in_specs=in_specs,
        out_specs=[pl.BlockSpec((tt, C), lambda b, i: (b * nT + i, 0)),
                   pl.BlockSpec((None, None, K - 1, C), lambda b, i: (l, b, 0, 0))],
        out_shape=[jax.ShapeDtypeStruct((M, C), BF16),
                   jax.ShapeDtypeStruct((L, B, K - 1, C), F32)],
        scratch_shapes=[pltpu.VMEM((nC, HIST_ROWS + tt, V7X_LANES), F32),
                        pltpu.VMEM((nC, tt, V7X_LANES), F32)],
        input_output_aliases=aliases,
        compiler_params=_cparams(2), name="conv_prompt")(*args)


def _conv_sample_kernel(a_ref, g_ref, cache_ref, w_ref, cb_ref, lg_ref, lb_ref,
                        o_ref, nb_ref, u_ref, upad_ref, z_ref, *, T, K, C, bb):
    nC = C // V7X_LANES
    lanes = [slice(c * V7X_LANES, (c + 1) * V7X_LANES) for c in range(nC)]
    u = _glu(a_ref, g_ref)
    for c in range(nC):
        u_ref[c] = u[:, lanes[c]]
    upad_ref[0:K - 1] = cache_ref[...]
    for t in range(T):
        for c in range(nC):
            upad_ref[K - 1 + t, :, lanes[c]] = u_ref[c, pl.ds(t, bb, stride=T), :]
    nb_ref[...] = upad_ref[T:T + K - 1]
    chunks = []
    for c in range(nC):
        acc = jnp.zeros((T, bb, V7X_LANES), F32)
        for j in range(K):
            acc = acc + upad_ref[j:j + T, :, lanes[c]] * w_ref[j:j + 1, lanes[c]][None]
        chunks.append(acc.reshape(T * bb, V7X_LANES) + cb_ref[:, lanes[c]])

    def write(c, v):
        z_ref[c] = v

    _conv_post(chunks, lg_ref, lb_ref, write, C)
    for b in range(0, bb, 2):
        for c in range(nC):
            pair = jnp.concatenate([z_ref[c, pl.ds(b, T, stride=bb), :],
                                    z_ref[c, pl.ds(b + 1, T, stride=bb), :]], axis=0)
            o_ref[b * T:(b + 2) * T, lanes[c]] = pair.astype(o_ref.dtype)


def conv_sample(proj, cache_t, l, w, cb, lg, lb, row0, B, T, C, K, hc, nb_prev):
    assert T % 8 == 0
    L = cache_t.shape[0]
    bb = _tile(B, 16, 8)
    assert row0 % (bb * T) == 0
    blk0 = row0 // (bb * T)
    kern = functools.partial(_conv_sample_kernel, T=T, K=K, C=C, bb=bb)
    vec = pl.BlockSpec((1, C), lambda s: (0, 0))
    in_specs = [pl.BlockSpec((bb * T, C), lambda s: (blk0 + s, 0)),
                pl.BlockSpec((bb * T, C), lambda s: (blk0 + s, 1)),
                pl.BlockSpec((None, K - 1, bb, C), lambda s: (l, 0, s, 0)),
                pl.BlockSpec((K, C), lambda s: (0, 0)), vec, vec, vec, ANY, ANY]
    args = [proj, proj, cache_t, w, cb, lg, lb, hc, nb_prev]
    aliases = {7: 0, 8: 1}
    return pl.pallas_call(
        _skip_refs(kern, 7, 2), grid=(B // bb,), in_specs=in_specs,
        out_specs=[pl.BlockSpec((bb * T, C), lambda s: (blk0 + s, 0)),
                   pl.BlockSpec((None, K - 1, bb, C), lambda s: (l, 0, s, 0))],
        out_shape=[jax.ShapeDtypeStruct(hc.shape, BF16),
                   jax.ShapeDtypeStruct((L, K - 1, B, C), F32)],
        scratch_shapes=[pltpu.VMEM((C // V7X_LANES, bb * T, V7X_LANES), F32),
                        pltpu.VMEM((K - 1 + T, bb, C), F32),
                        pltpu.VMEM((C // V7X_LANES, T * bb, V7X_LANES), F32)],
        input_output_aliases=aliases,
        compiler_params=_cparams(1), name="conv_sample")(*args)


def _retention_tables(H, chunk, rows):
    gamma = (1.0 - 2.0 ** (-5.0 - np.arange(H, dtype=np.float32))).astype(np.float32)
    log_g = np.log(gamma).astype(np.float32)
    idx = np.arange(chunk, dtype=np.float32)
    diff = idx[:, None] - idx[None, :]
    dmask = np.where(diff[None] >= 0, np.exp(diff[None] * log_g[:, None, None]), 0.0).astype(np.float32)
    q_dec = np.exp((idx[None, :] + np.float32(1.0)) * log_g[:, None]).astype(np.float32)
    k_dec = np.exp((np.float32(chunk) - np.float32(1.0) - idx[None, :]) * log_g[:, None]).astype(np.float32)
    c_dec = np.exp(np.float32(chunk) * log_g).astype(np.float32)
    dm = np.zeros((H, rows, rows), np.float32)
    dm[:, :chunk, :chunk] = dmask
    qd = np.zeros((H, rows, 1), np.float32)
    qd[:, :chunk, 0] = q_dec
    kd = np.zeros((H, rows, 1), np.float32)
    kd[:, :chunk, 0] = k_dec
    return dm, qd, kd, [float(c) for c in c_dec]


def _rotary_tables(pos, half):
    inv = (1.0 / (ROPE_BASE ** (np.arange(half, dtype=np.float32) / half))).astype(np.float32)
    ang = (pos.astype(np.float32)[:, None] * inv[None, :]).astype(np.float32)
    return np.cos(ang).astype(np.float32), np.sin(ang).astype(np.float32)


def _retention_kernel(*refs, H, DK, C, nb, cps, rows, c_dec, has_s0):
    if has_s0:
        (q_ref, k_ref, v_ref, g_ref, cos_ref, sin_ref, dm_ref, qd_ref, kd_ref, gng_ref, gnb_ref,
         s0_ref, o_ref, ns_ref) = refs
    else:
        (q_ref, k_ref, v_ref, g_ref, cos_ref, sin_ref, dm_ref, qd_ref, kd_ref, gng_ref, gnb_ref,
         o_ref, ns_ref) = refs
        s0_ref = None
    half = DK // 2
    scale = DK ** -0.5

    @pl.when(pl.program_id(1) == 0)
    def _():
        if has_s0:
            ns_ref[...] = s0_ref[...]
        else:
            ns_ref[...] = jnp.zeros(ns_ref.shape, F32)

    q_all = q_ref[...].astype(F32)
    k_all = k_ref[...].astype(F32)
    v_all = v_ref[...].astype(F32)
    g_all = g_ref[...].astype(F32)

    def rot(x, cos, sin):
        x1, x2 = x[:, :half], x[:, half:]
        return jnp.concatenate([x1 * cos - x2 * sin, x1 * sin + x2 * cos], axis=-1)

    def pad_rows(x):
        if rows == C:
            return x
        return jnp.concatenate([x, jnp.zeros((rows - C, x.shape[1]), x.dtype)], axis=0)

    outs = []
    for n in range(nb * cps):
        s, cc = divmod(n, cps)
        rs = slice(n * C, (n + 1) * C)
        cos = cos_ref[cc * C:(cc + 1) * C, :]
        sin = sin_ref[cc * C:(cc + 1) * C, :]
        heads = []
        for h in range(H):
            hs = slice(h * DK, (h + 1) * DK)
            qr = pad_rows(rot(q_all[rs, hs], cos, sin))
            kr = pad_rows(rot(k_all[rs, hs], cos, sin) * scale)
            vb = pad_rows(v_all[rs, hs]).astype(BF16)
            S = ns_ref[s, h]
            scores = lax.dot_general(qr.astype(BF16), kr.astype(BF16), (((1,), (1,)), ((), ())),
                                     preferred_element_type=F32) * dm_ref[h]
            o = (jnp.dot(scores.astype(BF16), vb, preferred_element_type=F32)
                 + jnp.dot((qr * qd_ref[h]).astype(BF16), S.astype(BF16), preferred_element_type=F32))
            kd = (kr * kd_ref[h]).astype(BF16)
            ns_ref[s, h] = c_dec[h] * S + lax.dot_general(kd, vb, (((0,), (0,)), ((), ())),
                                                          preferred_element_type=F32)
            o = o[:C]
            mu = jnp.mean(o, axis=-1, keepdims=True)
            d = o - mu
            var = jnp.mean(d * d, axis=-1, keepdims=True)
            on = d * lax.rsqrt(var + LN_EPS) * gng_ref[:, hs] + gnb_ref[:, hs]
            heads.append(_silu(g_all[rs, hs]) * on)
        outs.append(jnp.concatenate(heads, axis=-1))
    o_ref[...] = jnp.concatenate(outs, axis=0).astype(o_ref.dtype)


def retention(proj, gng, gnb, row0, B, T, H, DK, pos0, l, hr, ns_prev, state=None):
    M = proj.shape[0]
    L = ns_prev.shape[0]
    R = H * DK
    C = int(np.gcd(T, RET_CHUNK))
    n_chunks = T // C
    nb = 1 if n_chunks > 1 else _tile(B, RET_SEQS_PER_STEP, 1)
    cps = _tile(n_chunks, RET_CHUNKS_PER_STEP, 1)
    n_steps = n_chunks // cps
    rows = max(C, V7X_BF16_SUBLANES)
    dm, qd, kd, c_dec = _retention_tables(H, C, rows)
    cos, sin = _rotary_tables(pos0 + np.arange(T), DK // 2)
    blk_rows = nb * cps * C
    assert row0 % blk_rows == 0 and (blk_rows % V7X_BF16_SUBLANES == 0)
    blk0 = row0 // blk_rows
    has_s0 = state is not None
    kern = functools.partial(_retention_kernel, H=H, DK=DK, C=C, nb=nb, cps=cps, rows=rows,
                             c_dec=c_dec, has_s0=has_s0)

    def row_map(col):
        return lambda sb, c: (blk0 + sb * n_steps + c, col)

    in_specs = [pl.BlockSpec((blk_rows, R), row_map(2)),
                pl.BlockSpec((blk_rows, R), row_map(3)),
                pl.BlockSpec((blk_rows, R), row_map(4)),
                pl.BlockSpec((blk_rows, R), row_map(5)),
                pl.BlockSpec((cps * C, DK // 2), lambda sb, c: (c, 0)),
                pl.BlockSpec((cps * C, DK // 2), lambda sb, c: (c, 0)),
                pl.BlockSpec((H, rows, rows), lambda sb, c: (0, 0, 0)),
                pl.BlockSpec((H, rows, 1), lambda sb, c: (0, 0, 0)),
                pl.BlockSpec((H, rows, 1), lambda sb, c: (0, 0, 0)),
                pl.BlockSpec((1, R), lambda sb, c: (0, 0)),
                pl.BlockSpec((1, R), lambda sb, c: (0, 0))]
    args = [proj, proj, proj, proj, jnp.asarray(cos), jnp.asarray(sin), jnp.asarray(dm),
            jnp.asarray(qd), jnp.asarray(kd), gng, gnb]
    if has_s0:
        in_specs.append(pl.BlockSpec((None, nb, H, DK, DK), lambda sb, c: (l, sb, 0, 0, 0)))
        args.append(state)
    n_used = len(args)
    aliases = {n_used: 0, n_used + 1: 1}
    in_specs += [ANY, ANY]
    args += [hr, ns_prev]
    return pl.pallas_call(
        _skip_refs(kern, n_used, 2), grid=(B // nb, n_steps),
        in_specs=in_specs,
        out_specs=[pl.BlockSpec((blk_rows, R), lambda sb, c: (blk0 + sb * n_steps + c, 0)),
                   pl.BlockSpec((None, nb, H, DK, DK), lambda sb, c: (l, sb, 0, 0, 0))],
        out_shape=[jax.ShapeDtypeStruct((M, R), BF16),
                   jax.ShapeDtypeStruct((L, B, H, DK, DK), F32)],
        input_output_aliases=aliases,
        compiler_params=_cparams(2), name="retention")(*args)


def _outproj_ln_kernel(hc_ref, hr_ref, x_ref, w_ref, g_ref, b_ref, xo_ref, xb_ref, *, alpha, cc, sub):
    for s in range(x_ref.shape[0] // sub):
        rs = slice(s * sub, (s + 1) * sub)
        m = (jnp.dot(hc_ref[rs, :], w_ref[0:cc, :], preferred_element_type=F32)
             + jnp.dot(hr_ref[rs, :], w_ref[cc:, :], preferred_element_type=F32))
        z = _layer_norm(alpha * x_ref[rs, :] + m, g_ref[...], b_ref[...])
        xo_ref[rs, :] = z
        xb_ref[rs, :] = z.astype(BF16)


def outproj_ln(hc, hr, x, wb, g, b, alpha):
    M, D = x.shape
    cc = hc.shape[1]
    tm = _tile(M, 512, V7X_BF16_SUBLANES)
    kern = functools.partial(_outproj_ln_kernel, alpha=alpha, cc=cc, sub=_tile(tm, 128, V7X_BF16_SUBLANES))
    return pl.pallas_call(
        kern, grid=(M // tm,),
        in_specs=[pl.BlockSpec((tm, cc), lambda i: (i, 0)),
                  pl.BlockSpec((tm, hr.shape[1]), lambda i: (i, 0)),
                  pl.BlockSpec((tm, D), lambda i: (i, 0)),
                  pl.BlockSpec(wb.shape, lambda i: (0, 0)),
                  pl.BlockSpec((1, D), lambda i: (0, 0)),
                  pl.BlockSpec((1, D), lambda i: (0, 0))],
        out_specs=[pl.BlockSpec((tm, D), lambda i: (i, 0)),
                   pl.BlockSpec((tm, D), lambda i: (i, 0))],
        out_shape=[jax.ShapeDtypeStruct((M, D), F32), jax.ShapeDtypeStruct((M, D), BF16)],
        compiler_params=_cparams(1), name="outproj_ln")(hc, hr, x, wb, g, b)


def _ln_row_tile(M, n_first, pref):
    n = M if n_first is None else int(np.gcd(n_first, M - n_first))
    return _tile(n, pref, V7X_BF16_SUBLANES)


def _ln_outputs(M, D, tm, n_first, wrap):
    if n_first is None:
        row = pl.BlockSpec((tm, D), wrap(lambda i: (i, 0)))
        return [row, row], [jax.ShapeDtypeStruct((M, D), F32), jax.ShapeDtypeStruct((M, D), BF16)], None
    st = n_first // tm
    specs = [pl.BlockSpec((tm, D), wrap(lambda i: (jnp.minimum(i, st - 1), 0))),
             pl.BlockSpec((tm, D), wrap(lambda i: (jnp.maximum(i - st, 0), 0)))]
    return specs, [jax.ShapeDtypeStruct((n_first, D), F32), jax.ShapeDtypeStruct((M - n_first, D), F32)], st


def _ln_emit(z, o0_ref, o1_ref, split_tile):
    if split_tile is None:
        o0_ref[...] = z
        o1_ref[...] = z.astype(BF16)
    else:
        @pl.when(pl.program_id(0) < split_tile)
        def _():
            o0_ref[...] = z

        @pl.when(pl.program_id(0) >= split_tile)
        def _():
            o1_ref[...] = z


def _resid_ln_kernel(f_ref, x_ref, g_ref, b_ref, o0_ref, o1_ref, *, alpha, split_tile):
    z = _layer_norm(alpha * x_ref[...] + f_ref[...], g_ref[...], b_ref[...])
    _ln_emit(z, o0_ref, o1_ref, split_tile)


def resid_ln(f, x, g, b, alpha, n_first=None):
    M, D = x.shape
    tm = _ln_row_tile(M, n_first, 512)
    out_specs, out_shape, st = _ln_outputs(M, D, tm, n_first, lambda fn: fn)
    kern = functools.partial(_resid_ln_kernel, alpha=alpha, split_tile=st)
    row = pl.BlockSpec((tm, D), lambda i: (i, 0))
    vec = pl.BlockSpec((1, D), lambda i: (0, 0))
    return pl.pallas_call(
        kern, grid=(M // tm,), in_specs=[row, row, vec, vec], out_specs=out_specs, out_shape=out_shape,
        compiler_params=_cparams(1), name="resid_ln")(f, x, g, b)


def _tile_state(te_ref, na_ref):
    i = pl.program_id(1)
    active = i < na_ref[0]
    changed = jnp.logical_or(i == 0, te_ref[i] != te_ref[jnp.maximum(i - 1, 0)])
    return active, changed


def _weight_stream(te_ref, na_ref, tend_ref, copies, cast):
    j = pl.program_id(0)
    i = pl.program_id(1)
    active, changed = _tile_state(te_ref, na_ref)

    @pl.when(jnp.logical_and(active, changed))
    def _():
        e = te_ref[i]

        @pl.when(jnp.logical_and(j == 0, i == 0))
        def _():
            for c in copies(e, j):
                c.start()
        for c in copies(e, j):
            c.wait()
        cast()
        nxt = tend_ref[e]
        same_j = nxt < na_ref[0]
        e2 = te_ref[jnp.where(same_j, nxt, 0)]
        j2 = jnp.where(same_j, j, j + 1)

        @pl.when(j2 < pl.num_programs(0))
        def _():
            for c in copies(e2, j2):
                c.start()
    return active


def _row_tile_compute(rows, tm, sub, compute, o_ref):
    @pl.when(rows == tm)
    def _():
        o_ref[...] = compute(0, tm)

    @pl.when(rows < tm)
    def _():
        for s in range(tm // sub):
            @pl.when(s * sub < rows)
            def _():
                o_ref[s * sub:(s + 1) * sub, :] = compute(s * sub, sub)

            @pl.when(s * sub >= rows)
            def _():
                o_ref[s * sub:(s + 1) * sub, :] = jnp.zeros((sub, o_ref.shape[1]), o_ref.dtype)


def _gateup_kernel(te_ref, na_ref, tend_ref, tr_ref, x_ref, wg_hbm, wu_hbm, o_ref,
                   sg_ref, su_ref, wgb_ref, wub_ref, sem, *, li, tn, sub):
    def copies(e, j):
        cols = pl.ds(pl.multiple_of(j * tn, tn), tn)
        return (pltpu.make_async_copy(wg_hbm.at[li, e, :, cols], sg_ref, sem.at[0]),
                pltpu.make_async_copy(wu_hbm.at[li, e, :, cols], su_ref, sem.at[1]))

    def cast():
        wgb_ref[...] = sg_ref[...].astype(BF16)
        wub_ref[...] = su_ref[...].astype(BF16)

    _weight_stream(te_ref, na_ref, tend_ref, copies, cast)

    def compute(r0, n):
        x = x_ref[r0:r0 + n, :]
        gate = jnp.dot(x, wgb_ref[...], preferred_element_type=F32)
        up = jnp.dot(x, wub_ref[...], preferred_element_type=F32)
        return (_silu(gate) * up).astype(o_ref.dtype)

    _row_tile_compute(tr_ref[pl.program_id(1)], x_ref.shape[0], sub, compute, o_ref)


def _down_kernel(te_ref, na_ref, tend_ref, tr_ref, h_ref, w_hbm, o_ref, s_ref, wb_ref, sem, *, li, tn, sub):
    def copies(e, j):
        cols = pl.ds(pl.multiple_of(j * tn, tn), tn)
        return (pltpu.make_async_copy(w_hbm.at[li, e, :, cols], s_ref, sem.at[0]),)

    def cast():
        wb_ref[...] = s_ref[...].astype(BF16)

    _weight_stream(te_ref, na_ref, tend_ref, copies, cast)

    def compute(r0, n):
        return jnp.dot(h_ref[r0:r0 + n, :], wb_ref[...], preferred_element_type=F32).astype(o_ref.dtype)

    _row_tile_compute(tr_ref[pl.program_id(1)], h_ref.shape[0], sub, compute, o_ref)


def grouped_swiglu(xs, wg, wu, wd, li, tile_expert, n_active, tile_end, tile_rows, tm, out_dtype):
    S, D = xs.shape
    Fd = wg.shape[-1]
    nT = S // tm
    tn = _tile(Fd, 512, V7X_LANES)
    sub = _tile(tm, 128, V7X_BF16_SUBLANES)

    def row_map(j, i, te, na, tend, tr):
        return (jnp.minimum(i, na[0] - 1), 0)

    def out_map(j, i, te, na, tend, tr):
        return (i, j)

    h = pl.pallas_call(
        functools.partial(_gateup_kernel, li=li, tn=tn, sub=sub),
        grid_spec=pltpu.PrefetchScalarGridSpec(
            num_scalar_prefetch=4, grid=(Fd // tn, nT),
            in_specs=[pl.BlockSpec((tm, D), row_map), ANY, ANY],
            out_specs=pl.BlockSpec((tm, tn), out_map),
            scratch_shapes=[pltpu.VMEM((D, tn), F32), pltpu.VMEM((D, tn), F32),
                            pltpu.VMEM((D, tn), BF16), pltpu.VMEM((D, tn), BF16),
                            pltpu.SemaphoreType.DMA((2,))]),
        out_shape=jax.ShapeDtypeStruct((S, Fd), BF16),
        compiler_params=_cparams(2), name="ffn_gate_up")(tile_expert, n_active, tile_end, tile_rows, xs, wg, wu)

    tn2 = _tile(D, 512, V7X_LANES)
    return pl.pallas_call(
        functools.partial(_down_kernel, li=li, tn=tn2, sub=sub),
        grid_spec=pltpu.PrefetchScalarGridSpec(
            num_scalar_prefetch=4, grid=(D // tn2, nT),
            in_specs=[pl.BlockSpec((tm, Fd), row_map), ANY],
            out_specs=pl.BlockSpec((tm, tn2), out_map),
            scratch_shapes=[pltpu.VMEM((Fd, tn2), F32), pltpu.VMEM((Fd, tn2), BF16),
                            pltpu.SemaphoreType.DMA((1,))]),
        out_shape=jax.ShapeDtypeStruct((S, D), out_dtype),
        compiler_params=_cparams(2), name="ffn_down")(tile_expert, n_active, tile_end, tile_rows, h, wd)


def _router_kernel(x_ref, w_ref, b_ref, idx_ref, gate_ref):
    x = x_ref[...]
    w = w_ref[...]
    xh = x.astype(BF16)
    xl = (x - xh.astype(F32)).astype(BF16)
    wh = w.astype(BF16)
    wl = (w - wh.astype(F32)).astype(BF16)
    logits = (jnp.dot(xh, wh, preferred_element_type=F32) + jnp.dot(xh, wl, preferred_element_type=F32)
              + jnp.dot(xl, wh, preferred_element_type=F32)) + b_ref[...]
    col = lax.broadcasted_iota(jnp.int32, logits.shape, 1)
    big = jnp.int32(logits.shape[1])
    m1 = jnp.max(logits, axis=-1, keepdims=True)
    i1 = jnp.min(jnp.where(logits == m1, col, big), axis=-1, keepdims=True)
    rest = jnp.where(col == i1, -jnp.inf, logits)
    m2 = jnp.max(rest, axis=-1, keepdims=True)
    i2 = jnp.min(jnp.where(rest == m2, col, big), axis=-1, keepdims=True)
    e = jnp.exp(m2 - m1)
    g1 = 1.0 / (1.0 + e)
    g2 = e / (1.0 + e)
    idx_ref[...] = jnp.where(col == 0, i1, jnp.where(col == 1, i2, 0))
    gate_ref[...] = jnp.where(col == 0, g1, jnp.where(col == 1, g2, 0.0))


def router(x, w_pad, b_pad):
    M, D = x.shape
    tm = _tile(M, 512)
    out = pl.BlockSpec((tm, V7X_LANES), lambda i: (i, 0))
    return pl.pallas_call(
        _router_kernel, grid=(M // tm,),
        in_specs=[pl.BlockSpec((tm, D), lambda i: (i, 0)),
                  pl.BlockSpec((D, V7X_LANES), lambda i: (0, 0)),
                  pl.BlockSpec((1, V7X_LANES), lambda i: (0, 0))],
        out_specs=[out, out],
        out_shape=[jax.ShapeDtypeStruct((M, V7X_LANES), jnp.int32),
                   jax.ShapeDtypeStruct((M, V7X_LANES), F32)],
        compiler_params=_cparams(1), name="router")(x, w_pad, b_pad)


ISSUE_UNROLL = 8


def _dispatch_kernel(src_ref, na_ref, x_hbm, o_ref, buf_ref, sem, *, R):
    i = pl.program_id(0)
    na = na_ref[0]

    def issue_tile(t, slot):
        def body(r, carry):
            tok = src_ref[t * R + r]
            pltpu.make_async_copy(x_hbm.at[pl.ds(tok, 1), :], buf_ref.at[slot, pl.ds(r, 1), :],
                                  sem.at[slot]).start()
            return carry
        lax.fori_loop(0, R, body, 0, unroll=ISSUE_UNROLL)

    @pl.when(i == 0)
    def _():
        issue_tile(0, 0)

    @pl.when(i + 1 < na)
    def _():
        issue_tile(i + 1, (i + 1) % 2)

    @pl.when(i < na)
    def _():
        slot = i % 2
        pltpu.make_async_copy(x_hbm.at[pl.ds(0, R), :], buf_ref.at[slot], sem.at[slot]).wait()
        o_ref[...] = buf_ref[slot].astype(o_ref.dtype)

    @pl.when(i >= na)
    def _():
        o_ref[...] = jnp.zeros(o_ref.shape, o_ref.dtype)


def dispatch(x, src, n_active, S, R):
    M, D = x.shape
    kern = functools.partial(_dispatch_kernel, R=R)
    return pl.pallas_call(
        kern,
        grid_spec=pltpu.PrefetchScalarGridSpec(
            num_scalar_prefetch=2, grid=(S // R,),
            in_specs=[ANY],
            out_specs=pl.BlockSpec((R, D), lambda i, src, na: (i, 0)),
            scratch_shapes=[pltpu.VMEM((2, R, D), F32), pltpu.SemaphoreType.DMA((2,))]),
        out_shape=jax.ShapeDtypeStruct((S, D), BF16),
        compiler_params=_cparams(1), name="moe_dispatch")(src, n_active, x)


def _combine_ln_kernel(pos_ref, ys_hbm, x_ref, gate_ref, g_ref, b_ref, o0_ref, o1_ref, buf_ref, sem,
                       *, R, alpha, split_tile):
    i = pl.program_id(0)

    def issue_tile(t, slot):
        def body(r, carry):
            a = (t * R + r) * TOP_K
            for k in range(TOP_K):
                pltpu.make_async_copy(ys_hbm.at[pl.ds(pos_ref[a + k], 1), :],
                                      buf_ref.at[slot, k, pl.ds(r, 1), :], sem.at[slot]).start()
            return carry
        lax.fori_loop(0, R, body, 0, unroll=ISSUE_UNROLL // TOP_K)

    @pl.when(i == 0)
    def _():
        issue_tile(0, 0)

    @pl.when(i + 1 < pl.num_programs(0))
    def _():
        issue_tile(i + 1, (i + 1) % 2)

    slot = i % 2
    for k in range(TOP_K):
        pltpu.make_async_copy(ys_hbm.at[pl.ds(0, R), :], buf_ref.at[slot, k], sem.at[slot]).wait()
    gates = gate_ref[...]
    y = gates[:, 0:1] * buf_ref[slot, 0]
    for k in range(1, TOP_K):
        y = y + gates[:, k:k + 1] * buf_ref[slot, k]
    z = _layer_norm(alpha * x_ref[...] + y, g_ref[...], b_ref[...])
    _ln_emit(z, o0_ref, o1_ref, split_tile)


def combine_ln(ys, pos, x, gates, g, b, alpha, n_first=None):
    M, D = x.shape
    R = _ln_row_tile(M, n_first, 256)
    out_specs, out_shape, st = _ln_outputs(M, D, R, n_first, lambda fn: (lambda i, pos: fn(i)))
    kern = functools.partial(_combine_ln_kernel, R=R, alpha=alpha, split_tile=st)
    row = lambda i, pos: (i, 0)
    fix = lambda i, pos: (0, 0)
    return pl.pallas_call(
        kern,
        grid_spec=pltpu.PrefetchScalarGridSpec(
            num_scalar_prefetch=1, grid=(M // R,),
            in_specs=[ANY,
                      pl.BlockSpec((R, D), row),
                      pl.BlockSpec((R, V7X_LANES), row),
                      pl.BlockSpec((1, D), fix),
                      pl.BlockSpec((1, D), fix)],
            out_specs=out_specs,
            scratch_shapes=[pltpu.VMEM((2, TOP_K, R, D), F32), pltpu.SemaphoreType.DMA((2,))]),
        out_shape=out_shape,
        compiler_params=_cparams(1), name="moe_combine_ln")(pos, ys, x, gates, g, b)


def _routing_plan(idx, E, tm, n_tiles):
    e_flat = idx.reshape(-1)
    A = e_flat.shape[0]
    a_ids = jnp.arange(A, dtype=jnp.int32)
    order = jnp.sort(e_flat * A + a_ids) % A
    rank = jnp.argsort(order).astype(jnp.int32)
    counts = jnp.sum((e_flat[:, None] == jnp.arange(E, dtype=jnp.int32)[None, :]).astype(jnp.int32), axis=0)
    tiles_e = (counts + tm - 1) // tm
    tile_end = jnp.cumsum(tiles_e).astype(jnp.int32)
    tile_start = tile_end - tiles_e
    pad_start = tile_start * tm
    start = jnp.cumsum(counts) - counts
    n_active = tile_end[-1]
    pos = pad_start[e_flat] + rank - start[e_flat]
    tile_id = jnp.arange(n_tiles, dtype=jnp.int32)
    te = jnp.sum((jnp.minimum(tile_id, n_active - 1)[:, None] >= tile_end[None, :]).astype(jnp.int32), axis=1)
    tile_rows = jnp.where(tile_id < n_active,
                          jnp.clip(counts[te] - (tile_id - tile_start[te]) * tm, 0, tm), 0)
    slot = jnp.arange(n_tiles * tm, dtype=jnp.int32)
    se = te[slot // tm]
    off = slot - pad_start[se]
    r = start[se] + jnp.clip(off, 0, jnp.maximum(counts[se] - 1, 0))
    src = order[jnp.clip(r, 0, A - 1)] // TOP_K
    return (src.astype(jnp.int32), pos.astype(jnp.int32), te,
            n_active.reshape(1).astype(jnp.int32), tile_end, tile_rows.astype(jnp.int32))


def kernel(x_prompt, x_sample, cache_conv, state_ret, w_in, conv_w, conv_b, conv_ln_g, conv_ln_b,
           ret_gn_g, ret_gn_b, w_out, ln1_g, ln1_b, ln2_g, ln2_b, w_ff_gate, w_ff_up, w_ff_down,
           w_router, b_router, w_exp_gate, w_exp_up, w_exp_down):
    B, T, D = x_prompt.shape
    Bs, Ts, _ = x_sample.shape
    L = w_in.shape[0]
    K = conv_w.shape[1]
    C = conv_w.shape[2]
    H, DK = state_ret.shape[2], state_ret.shape[3]
    E = w_router.shape[-1]
    Np, Ns = B * T, Bs * Ts
    M = Np + Ns
    nC = C // V7X_LANES
    alpha = float((2 * L) ** 0.25)
    assert w_in.shape[-1] == 6 * C and H * DK == C and D == 2 * C

    x, xb = pack_tokens(x_prompt.reshape(Np, D), x_sample.reshape(Ns, D))

    tm_dense = _tile(M, 1024, V7X_BF16_SUBLANES)
    n_dense_tiles = M // tm_dense
    dense_te = jnp.zeros((n_dense_tiles,), jnp.int32)
    dense_na = jnp.full((1,), n_dense_tiles, jnp.int32)
    dense_rows = jnp.full((n_dense_tiles,), tm_dense, jnp.int32)
    w_ff_gate4, w_ff_up4, w_ff_down4 = w_ff_gate[:, None], w_ff_up[:, None], w_ff_down[:, None]
    moe_tm = _tile(M * TOP_K, MOE_TILE, V7X_BF16_SUBLANES)
    moe_tiles = (M * TOP_K) // moe_tm + E

    hc = jnp.zeros((M, C), BF16)
    hr = jnp.zeros((M, H * DK), BF16)
    conv_p = jnp.zeros((L, B, K - 1, C), F32)
    conv_s = jnp.zeros((L, K - 1, Bs, C), F32)
    cache_t = cache_conv.transpose(0, 2, 1, 3)
    ret_p = jnp.zeros((L, B, H, DK, DK), F32)
    ret_s = jnp.zeros((L, Bs, H, DK, DK), F32)
    for l in range(L):
        proj = matmul_ws(xb, w_in, l, BF16)

        w3 = conv_w[l].reshape(K, nC, V7X_LANES).transpose(1, 0, 2)
        cb3 = conv_b[l].reshape(nC, 1, V7X_LANES)
        cb, lg, lb = conv_b[l][None], conv_ln_g[l][None], conv_ln_b[l][None]
        hc, conv_p = conv_prompt(proj, w3, cb3, lg, lb, B, T, C, K, l, hc, conv_p)
        hc, conv_s = conv_sample(proj, cache_t, l, conv_w[l], cb, lg, lb, Np, Bs, Ts, C, K, hc, conv_s)

        gng, gnb = ret_gn_g[l][None], ret_gn_b[l][None]
        hr, ret_p = retention(proj, gng, gnb, 0, B, T, H, DK, 0, l, hr, ret_p)
        hr, ret_s = retention(proj, gng, gnb, Np, Bs, Ts, H, DK, PAST_LEN, l, hr, ret_s, state=state_ret)

        x, xb = outproj_ln(hc, hr, x, cast_weight(w_out, l), ln1_g[l][None], ln1_b[l][None], alpha)
        n_first = Np if l == L - 1 else None

        if l % 2 == 0:
            f = grouped_swiglu(xb, w_ff_gate4, w_ff_up4, w_ff_down4, l // 2, dense_te, dense_na,
                               dense_na, dense_rows, tm_dense, BF16)
            x, xb = resid_ln(f, x, ln2_g[l][None], ln2_b[l][None], alpha, n_first)
        else:
            li = l // 2
            w_pad = jnp.zeros((D, V7X_LANES), F32).at[:, :E].set(w_router[li])
            b_pad = jnp.full((1, V7X_LANES), -1e30, F32).at[0, :E].set(b_router[li])
            idx_w, gate_w = router(x, w_pad, b_pad)
            src, pos, te, na, tend, trows = _routing_plan(idx_w[:, :TOP_K], E, moe_tm, moe_tiles)
            xs = dispatch(x, src, na, moe_tiles * moe_tm, moe_tm)
            ys = grouped_swiglu(xs, w_exp_gate, w_exp_up, w_exp_down, li, te, na, tend, trows, moe_tm, F32)
            x, xb = combine_ln(ys, pos, x, gate_w, ln2_g[l][None], ln2_b[l][None], alpha, n_first)

    return (x.reshape(B, T, D), xb.reshape(Bs, Ts, D), conv_p, ret_p, conv_s.transpose(0, 2, 1, 3), ret_s)
```
